```python
import math
import jax, jax.numpy as jnp
from jax import lax
import numpy as np

D_MODEL = 1024
BATCH = 8
SEQ = 4096
DEPTH = 1

CHUNK = 64
Q_BLOCK = 128
SB_HEAD_DIM = 64
SB_HEADS = D_MODEL // 128
SB_WIDTH = SB_HEADS * SB_HEAD_DIM
SSM_WIDTH = D_MODEL // 2
SSM_GROUP = 16
SSM_GROUPS = SSM_WIDTH // SSM_GROUP
SSM_STATE = 64
N_MEM = 256
XA_HEADS = 4
XA_HEAD_DIM = D_MODEL // XA_HEADS
D_FF = 11 * D_MODEL // 4
CONV_WIDTH = 3
RMS_EPS = 1e-6
IN_WIDTH = 3 * SB_WIDTH + SSM_WIDTH + 2 * D_MODEL

kernel_name = "hybrid_stickbreak_s5_memxattn_convffn"


def rms_norm(x, gain):
    xf = x.astype(jnp.float32)
    y = xf * lax.rsqrt(jnp.mean(xf * xf, axis=-1, keepdims=True) + RMS_EPS)
    return (y * gain.astype(jnp.float32)).astype(x.dtype)


def stick_breaking_attention(q, k, v):
    b, s, _ = q.shape
    n_blk = s // Q_BLOCK
    qh = q.reshape(b, n_blk, Q_BLOCK, SB_HEADS, SB_HEAD_DIM).transpose(1, 0, 3, 2, 4)
    kh = k.reshape(b, s, SB_HEADS, SB_HEAD_DIM).transpose(0, 2, 1, 3)
    vh = v.reshape(b, s, SB_HEADS, SB_HEAD_DIM).transpose(0, 2, 1, 3)
    key_pos = jnp.arange(s, dtype=jnp.int32)
    scale = SB_HEAD_DIM ** -0.5

    def one_block(args):
        q_blk, blk = args
        z = jnp.einsum('bhqd,bhkd->bhqk', q_blk, kh).astype(jnp.float32) * scale
        q_pos = blk * Q_BLOCK + jnp.arange(Q_BLOCK, dtype=jnp.int32)
        causal = key_pos[None, :] < q_pos[:, None]
        log_fail = jnp.where(causal, jax.nn.log_sigmoid(-z), 0.0)
        log_remain = lax.cumsum(log_fail, axis=3, reverse=True) - log_fail
        w = jnp.where(causal, jnp.exp(jax.nn.log_sigmoid(z) + log_remain), 0.0)
        return jnp.einsum('bhqk,bhkd->bhqd', w.astype(vh.dtype), vh)

    out = lax.map(one_block, (qh, jnp.arange(n_blk, dtype=jnp.int32)))
    return out.transpose(1, 0, 3, 2, 4).reshape(b, s, SB_WIDTH)


def _linear_recurrence_combine(e_i, e_j):
    a_i, b_i = e_i
    a_j, b_j = e_j
    return a_j * a_i, a_j * b_i + b_j


def s5_branch(u, a_re, a_im, log_dt, b_re, b_im, c_re, c_im, d_skip, w_glu, b_glu):
    b, s, _ = u.shape
    f32 = jnp.float32
    ug = u.astype(f32).reshape(b, s, SSM_GROUPS, SSM_GROUP)
    lam = lax.complex(a_re.astype(f32), a_im.astype(f32))
    dt = jnp.exp(log_dt.astype(f32))[:, None]
    lam_bar = jnp.exp(lam * dt)
    b_mat = lax.complex(b_re.astype(f32), b_im.astype(f32))
    b_bar = ((lam_bar - 1.0) / lam)[:, :, None] * b_mat
    bu = jnp.einsum('gpc,bsgc->bsgp', b_bar, ug.astype(jnp.complex64))
    decay = jnp.broadcast_to(lam_bar, (1, s, SSM_GROUPS, SSM_STATE))
    _, states = lax.associative_scan(_linear_recurrence_combine, (decay, bu), axis=1)
    c_mat = lax.complex(c_re.astype(f32), c_im.astype(f32))
    y = jnp.einsum('gcp,bsgp->bsgc', c_mat, states).real
    y = y + d_skip.astype(f32).reshape(SSM_GROUPS, SSM_GROUP) * ug
    y = jax.nn.gelu(y.reshape(b, s, SSM_WIDTH)).astype(u.dtype)
    return y * jax.nn.sigmoid(y @ w_glu + b_glu)


def memory_cross_attention(h, mem_n, wq, wk, wv, wo):
    b, s, _ = h.shape
    m = mem_n.shape[1]
    q = (h @ wq).reshape(b, s, XA_HEADS, XA_HEAD_DIM)
    k = (mem_n @ wk).reshape(b, m, XA_HEADS, XA_HEAD_DIM)
    v = (mem_n @ wv).reshape(b, m, XA_HEADS, XA_HEAD_DIM)
    scores = jnp.einsum('bqhd,bmhd->bhqm', q, k).astype(jnp.float32) * (XA_HEAD_DIM ** -0.5)
    p = jax.nn.softmax(scores, axis=-1).astype(v.dtype)
    o = jnp.einsum('bhqm,bmhd->bqhd', p, v).reshape(b, s, D_MODEL)
    return o @ wo


def conv_ffn(h, w_up, conv_w, conv_b, w_down):
    up = h @ w_up
    up = lax.conv_general_dilated(
        up, conv_w[:, None, :].astype(up.dtype), window_strides=(1,),
        padding=[(CONV_WIDTH - 1, 0)], dimension_numbers=('NWC', 'WIO', 'NWC'),
        feature_group_count=2 * D_FF) + conv_b
    gate, val = jnp.split(up, 2, axis=-1)
    return (jax.nn.gelu(gate) * val) @ w_down


def _normal(key, shape, scale):
    return jax.random.normal(key, shape, jnp.float32) * scale


def setup_inputs(seed: int = 0) -> dict:
    key = jax.random.key(seed)
    ks = jax.random.split(key, 32)
    L, D, G, P, C = DEPTH, D_MODEL, SSM_GROUPS, SSM_STATE, SSM_GROUP
    gain = lambda k: 1.0 + _normal(k, (L, D), 0.05)
    return {
        "x": _normal(ks[0], (BATCH, SEQ, D), 1.0),
        "mem": _normal(ks[1], (BATCH, N_MEM, D), 1.0),
        "norm_mix_pre": gain(ks[2]),
        "norm_mix_post": gain(ks[3]),
        "w_in": _normal(ks[4], (L, D, IN_WIDTH), D ** -0.5),
        "b_gate": _normal(ks[5], (L, 2 * D), 0.01),
        "ssm_a_re": -0.5 + _normal(ks[6], (L, G, P), 0.01),
        "ssm_a_im": jnp.pi * jnp.arange(P, dtype=jnp.float32)[None, None, :] + _normal(ks[7], (L, G, P), 0.01),
        "ssm_log_dt": jax.random.uniform(ks[8], (L, G), jnp.float32, math.log(1e-3), math.log(1e-1)),
        "ssm_b_re": _normal(ks[9], (L, G, P, C), (2 * C) ** -0.5),
        "ssm_b_im": _normal(ks[10], (L, G, P, C), (2 * C) ** -0.5),
        "ssm_c_re": _normal(ks[11], (L, G, C, P), (2 * P) ** -0.5),
        "ssm_c_im": _normal(ks[12], (L, G, C, P), (2 * P) ** -0.5),
        "ssm_d": _normal(ks[13], (L, SSM_WIDTH), 1.0),
        "ssm_w_glu": _normal(ks[14], (L, SSM_WIDTH, SSM_WIDTH), SSM_WIDTH ** -0.5),
        "ssm_b_glu": _normal(ks[15], (L, SSM_WIDTH), 0.01),
        "w_branch_attn": _normal(ks[16], (L, SB_WIDTH, D), SB_WIDTH ** -0.5),
        "w_branch_ssm": _normal(ks[17], (L, SSM_WIDTH, D), SSM_WIDTH ** -0.5),
        "w_out": _normal(ks[18], (L, D, D), D ** -0.5),
        "norm_xa_pre": gain(ks[19]),
        "norm_xa_post": gain(ks[20]),
        "norm_mem": gain(ks[21]),
        "xa_wq": _normal(ks[22], (L, D, D), D ** -0.5),
        "xa_wk": _normal(ks[23], (L, D, D), D ** -0.5),
        "xa_wv": _normal(ks[24], (L, D, D), D ** -0.5),
        "xa_wo": _normal(ks[25], (L, D, D), D ** -0.5),
        "norm_ffn_pre": gain(ks[26]),
        "norm_ffn_post": gain(ks[27]),
        "ffn_w_up": _normal(ks[28], (L, D, 2 * D_FF), D ** -0.5),
        "ffn_conv_w": _normal(ks[29], (L, CONV_WIDTH, 2 * D_FF), CONV_WIDTH ** -0.5),
        "ffn_conv_b": _normal(ks[30], (L, 2 * D_FF), 0.01),
        "ffn_w_down": _normal(ks[31], (L, D_FF, D), D_FF ** -0.5),
    }


def reference(x, mem, norm_mix_pre, norm_mix_post, w_in, b_gate,
              ssm_a_re, ssm_a_im, ssm_log_dt, ssm_b_re, ssm_b_im, ssm_c_re, ssm_c_im,
              ssm_d, ssm_w_glu, ssm_b_glu, w_branch_attn, w_branch_ssm, w_out,
              norm_xa_pre, norm_xa_post, norm_mem, xa_wq, xa_wk, xa_wv, xa_wo,
              norm_ffn_pre, norm_ffn_post, ffn_w_up, ffn_conv_w, ffn_conv_b, ffn_w_down):
    splits = (SB_WIDTH, 2 * SB_WIDTH, 3 * SB_WIDTH, 3 * SB_WIDTH + SSM_WIDTH)
    for l in range(DEPTH):
        h = rms_norm(x, norm_mix_pre[l])
        proj = h @ w_in[l]
        q, k, v, u, gate_logits = jnp.split(proj, splits, axis=-1)
        gate_attn, gate_ssm = jnp.split(jax.nn.sigmoid(gate_logits + b_gate[l]), 2, axis=-1)
        o_attn = stick_breaking_attention(q, k, v)
        o_ssm = s5_branch(u, ssm_a_re[l], ssm_a_im[l], ssm_log_dt[l], ssm_b_re[l], ssm_b_im[l],
                          ssm_c_re[l], ssm_c_im[l], ssm_d[l], ssm_w_glu[l], ssm_b_glu[l])
        merged = gate_attn * (o_attn @ w_branch_attn[l]) + gate_ssm * (o_ssm @ w_branch_ssm[l])
        x = x + rms_norm(merged @ w_out[l], norm_mix_post[l])
        h = rms_norm(x, norm_xa_pre[l])
        mem_n = rms_norm(mem, norm_mem[l])
        xa = memory_cross_attention(h, mem_n, xa_wq[l], xa_wk[l], xa_wv[l], xa_wo[l])
        x = x + rms_norm(xa, norm_xa_post[l])
        h = rms_norm(x, norm_ffn_pre[l])
        f = conv_ffn(h, ffn_w_up[l], ffn_conv_w[l], ffn_conv_b[l], ffn_w_down[l])
        x = x + rms_norm(f, norm_ffn_post[l])
    return x
```

```python
import functools
import math

import jax
import jax.numpy as jnp
from jax import lax
from jax.experimental import pallas as pl
from jax.experimental.pallas import tpu as pltpu

F32 = jnp.float32
BF16 = jnp.bfloat16

RMS_EPS = 1e-6
SB_HEAD_DIM = 64
SSM_GROUP = 16
SSM_STATE = 64
XA_HEADS = 4
CONV_WIDTH = 3
LANES = 128
SUBLANES = 8
VMEM_LIMIT = 56 * 1024 * 1024

LOG2E = 1.4426950408889634


def _rms(x, g):
    ms = jnp.mean(x * x, axis=-1, keepdims=True)
    return x * lax.rsqrt(ms + RMS_EPS) * g


def _dot(a, b):
    return jnp.dot(a, b, preferred_element_type=F32)


def _dot_nt(a, b):
    return lax.dot_general(a, b, (((1,), (1,)), ((), ())), preferred_element_type=F32)


def _params(n_axes):
    return pltpu.CompilerParams(
        dimension_semantics=("arbitrary",) * n_axes, vmem_limit_bytes=VMEM_LIMIT)


def _const_spec(shape):
    zeros = (0,) * len(shape)
    return pl.BlockSpec(shape, lambda *_: zeros, pipeline_mode=pl.Buffered(1))


def _discretize_kernel(are_ref, aim_ref, ldt_ref, bre_ref, bim_ref,
                       lre_ref, lim_ref, bbre_ref, bbim_ref):
    a_re = are_ref[...]
    a_im = aim_ref[...]
    dt = jnp.exp(ldt_ref[...])
    mag = jnp.exp(a_re * dt)
    l_re = mag * jnp.cos(a_im * dt)
    l_im = mag * jnp.sin(a_im * dt)
    lre_ref[...] = l_re
    lim_ref[...] = l_im
    n_re = l_re - 1.0
    inv = 1.0 / (a_re * a_re + a_im * a_im)
    c_re = (n_re * a_re + l_im * a_im) * inv
    c_im = (l_im * a_re - n_re * a_im) * inv
    b_re = bre_ref[...]
    b_im = bim_ref[...]
    bbre_ref[...] = c_re * b_re - c_im * b_im
    bbim_ref[...] = c_re * b_im + c_im * b_re


def _discretize(a_re, a_im, log_dt, b_re, b_im):
    g, p, c = b_re.shape
    rep = lambda a: jnp.repeat(a, c, axis=-1)
    args = (rep(a_re), rep(a_im), jnp.broadcast_to(log_dt[:, None], (g, p * c)),
            b_re.reshape(g, p * c), b_im.reshape(g, p * c))
    out = jax.ShapeDtypeStruct((g, p * c), F32)
    l_re, l_im, bb_re, bb_im = pl.pallas_call(
        _discretize_kernel, out_shape=(out, out, out, out), name="ssm_discretize")(*args)
    return (l_re[:, ::c], l_im[:, ::c], bb_re.reshape(g, p, c), bb_im.reshape(g, p, c))


def _inproj_kernel(x_ref, g_ref, w_ref, bg_ref, q_ref, k_ref, v_ref, u_ref, ga_ref, gs_ref,
                   *, sbw, ssw, d, qscale):
    h = _rms(x_ref[...], g_ref[...]).astype(BF16)

    def proj(c0, n):
        return _dot(h, w_ref[:, c0:c0 + n])

    q_ref[...] = (proj(0, sbw) * qscale).astype(BF16)
    k_ref[...] = proj(sbw, sbw).astype(BF16)
    v_ref[...] = proj(2 * sbw, sbw).astype(BF16)
    u_ref[...] = proj(3 * sbw, ssw).astype(BF16)
    g0 = 3 * sbw + ssw
    cw = 512
    for c in range(d // cw):
        sl = slice(c * cw, (c + 1) * cw)
        ga_ref[:, sl] = jax.nn.sigmoid(proj(g0 + c * cw, cw) + bg_ref[:, sl]).astype(BF16)
        gs_ref[:, sl] = jax.nn.sigmoid(
            proj(g0 + d + c * cw, cw) + bg_ref[:, d + c * cw:d + (c + 1) * cw]).astype(BF16)


def _in_proj(x, gain, w_in, b_gate, *, sbw, ssw, ts):
    b, s, d = x.shape
    in_w = w_in.shape[1]
    qscale = SB_HEAD_DIM ** -0.5 * LOG2E
    tok = lambda w: pl.BlockSpec((None, ts, w), lambda bi, ti: (bi, ti, 0))
    out_shape = (
        jax.ShapeDtypeStruct((b, s, sbw), BF16),
        jax.ShapeDtypeStruct((b, s, sbw), BF16),
        jax.ShapeDtypeStruct((b, s, sbw), BF16),
        jax.ShapeDtypeStruct((s, b * ssw), BF16),
        jax.ShapeDtypeStruct((b, s, d), BF16),
        jax.ShapeDtypeStruct((b, s, d), BF16),
    )
    return pl.pallas_call(
        functools.partial(_inproj_kernel, sbw=sbw, ssw=ssw, d=d, qscale=qscale),
        grid=(b, s // ts),
        in_specs=[tok(d), _const_spec((1, d)), _const_spec((d, in_w)), _const_spec((1, 2 * d))],
        out_specs=(tok(sbw), tok(sbw), tok(sbw),
                   pl.BlockSpec((ts, ssw), lambda bi, ti: (ti, bi)),
                   tok(d), tok(d)),
        out_shape=out_shape,
        compiler_params=_params(2),
        name="in_proj",
    )(x, gain, w_in, b_gate)


def _sb_attn_kernel(q_ref, k_ref, v_ref, o_ref, acc_ref, r_ref, *, tq):
    tk = tq
    qi = pl.program_id(2)
    q = q_ref[...]
    lane = lax.broadcasted_iota(jnp.int32, (tq, LANES), 1)
    qh = [jnp.where((lane // SB_HEAD_DIM) == h, q, jnp.zeros_like(q)) for h in range(2)]
    row = lax.broadcasted_iota(jnp.int32, (tq, tk), 0)
    col = lax.broadcasted_iota(jnp.int32, (tq, tk), 1)
    causal = col < row
    ucat = jnp.concatenate(
        [(row >= col).astype(BF16), jnp.ones((tk, LANES), BF16)], axis=1)

    acc_ref[...] = jnp.zeros_like(acc_ref)
    r_ref[...] = jnp.zeros_like(r_ref)

    def block(kb, masked):
        ks = pl.multiple_of(kb * tk, tk)
        kblk = k_ref[pl.ds(ks, tk), :]
        vblk = v_ref[pl.ds(ks, tk), :]
        for h in range(2):
            z = _dot_nt(qh[h], kblk)
            sp = jnp.maximum(z, 0.0) + jnp.log2(1.0 + jnp.exp2(-jnp.abs(z)))
            if masked:
                sp = jnp.where(causal, sp, 0.0)
            hi = sp.astype(BF16)
            lo = (sp - hi.astype(F32)).astype(BF16)
            ct = _dot(hi, ucat) + _dot(lo, ucat)
            r = r_ref[h]
            w = jnp.exp2(z - ct[:, :tk] - r)
            if masked:
                w = jnp.where(causal, w, 0.0)
            acc_ref[h] += _dot(w.astype(BF16), vblk)
            r_ref[h] = r + ct[:, tk:]

    block(qi, True)

    def body(i, carry):
        block(qi - 1 - i, False)
        return carry

    lax.fori_loop(0, qi, body, 0)
    o_ref[...] = jnp.where(lane < SB_HEAD_DIM, acc_ref[0], acc_ref[1]).astype(o_ref.dtype)


def _sb_attn(q, k, v, *, tq):
    b, s, w = q.shape
    hp = w // LANES
    return pl.pallas_call(
        functools.partial(_sb_attn_kernel, tq=tq),
        grid=(b, hp, s // tq),
        in_specs=[
            pl.BlockSpec((None, tq, LANES), lambda bi, hi, qi: (bi, qi, hi)),
            pl.BlockSpec((None, s, LANES), lambda bi, hi, qi: (bi, 0, hi)),
            pl.BlockSpec((None, s, LANES), lambda bi, hi, qi: (bi, 0, hi)),
        ],
        out_specs=pl.BlockSpec((None, tq, LANES), lambda bi, hi, qi: (bi, qi, hi)),
        out_shape=jax.ShapeDtypeStruct((b, s, w), BF16),
        scratch_shapes=[pltpu.VMEM((2, tq, LANES), F32), pltpu.VMEM((2, tq, LANES), F32)],
        compiler_params=_params(3),
        name="sb_attn",
    )(q, k, v)


def _ssm_kernel(u_ref, bs_ref, are_ref, aim_ref, cs_ref, d_ref, wg_ref, bg_ref, o_ref,
                bu_ref, xs_ref, *, t_steps, n_state):
    nb = SUBLANES
    n_slab = u_ref.shape[1] // LANES
    sw = n_state // n_slab

    @pl.when(pl.program_id(0) == 0)
    def _():
        xs_ref[...] = jnp.zeros_like(xs_ref)

    u = u_ref[...]
    for s in range(n_slab):
        bu = _dot(u[:, s * LANES:(s + 1) * LANES], bs_ref[s])
        bu_ref[:, s * sw:(s + 1) * sw] = bu[:, :sw]
        bu_ref[:, n_state + s * sw:n_state + (s + 1) * sw] = bu[:, sw:]

    n_part = 2
    pw = n_state // n_part
    for part in range(n_part):
        re = slice(part * pw, (part + 1) * pw)
        im = slice(n_state + part * pw, n_state + (part + 1) * pw)
        a_re = are_ref[:, re]
        a_im = aim_ref[:, re]

        def body(t, carry, re=re, im=im, a_re=a_re, a_im=a_im):
            x_re, x_im = carry
            rows = pl.ds(pl.multiple_of(t * nb, nb), nb)
            n_re = a_re * x_re - a_im * x_im + bu_ref[rows, re]
            n_im = a_re * x_im + a_im * x_re + bu_ref[rows, im]
            bu_ref[rows, re] = n_re
            bu_ref[rows, im] = n_im
            return n_re, n_im

        x_re, x_im = lax.fori_loop(0, t_steps, body, (xs_ref[:, re], xs_ref[:, im]), unroll=2)
        xs_ref[:, re] = x_re
        xs_ref[:, im] = x_im

    ys = []
    for s in range(n_slab):
        x_re = bu_ref[:, s * sw:(s + 1) * sw].astype(BF16)
        x_im = bu_ref[:, n_state + s * sw:n_state + (s + 1) * sw].astype(BF16)
        ys.append(_dot(x_re, cs_ref[s, :sw, :]) + _dot(x_im, cs_ref[s, sw:, :]))
    y = jnp.concatenate(ys, axis=1) + d_ref[...] * u.astype(F32)
    y = jax.nn.gelu(y, approximate=True)
    gate = jax.nn.sigmoid(_dot(y.astype(BF16), wg_ref[...]) + bg_ref[...])
    o_ref[...] = (y * gate).astype(o_ref.dtype)


def _ssm(u_tm, b_slab, lam_re, lam_im, c_slab, d_skip, w_glu, b_glu, *, batch, t_steps):
    rows, ssw = u_tm.shape
    n_state = lam_re.shape[1]
    assert batch == SUBLANES, "time-major scan puts the batch on the sublanes"
    blk = t_steps * batch
    return pl.pallas_call(
        functools.partial(_ssm_kernel, t_steps=t_steps, n_state=n_state),
        grid=(rows // blk,),
        in_specs=[
            pl.BlockSpec((blk, ssw), lambda i: (i, 0)),
            _const_spec(b_slab.shape), _const_spec(lam_re.shape), _const_spec(lam_im.shape),
            _const_spec(c_slab.shape), _const_spec(d_skip.shape), _const_spec(w_glu.shape),
            _const_spec(b_glu.shape),
        ],
        out_specs=pl.BlockSpec((blk, ssw), lambda i: (i, 0)),
        out_shape=jax.ShapeDtypeStruct((rows, ssw), BF16),
        scratch_shapes=[pltpu.VMEM((blk, 2 * n_state), F32), pltpu.VMEM((batch, 2 * n_state), F32)],
        compiler_params=_params(1),
        name="ssm",
    )(u_tm, b_slab, lam_re, lam_im, c_slab, d_skip, w_glu, b_glu)


def _merge_kernel(oa_ref, os_ref, ga_ref, gs_ref, x_ref, wa_ref, ws_ref, wo_ref, g_ref, out_ref):
    pa = _dot(oa_ref[...], wa_ref[...])
    ps = _dot(os_ref[...], ws_ref[...])
    merged = ga_ref[...].astype(F32) * pa + gs_ref[...].astype(F32) * ps
    y = _dot(merged.astype(BF16), wo_ref[...])
    out_ref[...] = x_ref[...] + _rms(y, g_ref[...])


def _merge_out(o_attn, o_ssm_tm, ga, gs, x, wa, ws, wo, gain, *, ts):
    b, s, d = x.shape
    sbw = o_attn.shape[2]
    ssw = o_ssm_tm.shape[1] // b
    tok = lambda w: pl.BlockSpec((None, ts, w), lambda bi, ti: (bi, ti, 0))
    return pl.pallas_call(
        _merge_kernel,
        grid=(b, s // ts),
        in_specs=[tok(sbw), pl.BlockSpec((ts, ssw), lambda bi, ti: (ti, bi)), tok(d), tok(d), tok(d),
                  _const_spec(wa.shape), _const_spec(ws.shape), _const_spec(wo.shape),
                  _const_spec(gain.shape)],
        out_specs=tok(d),
        out_shape=jax.ShapeDtypeStruct((b, s, d), F32),
        compiler_params=_params(2),
        name="merge_out",
    )(o_attn, o_ssm_tm, ga, gs, x, wa, ws, wo, gain)


def _memkv_kernel(m_ref, g_ref, wk_ref, wv_ref, k_ref, v_ref):
    mn = _rms(m_ref[...], g_ref[...]).astype(BF16)
    k_ref[...] = _dot(mn, wk_ref[...]).astype(BF16)
    v_ref[...] = _dot(mn, wv_ref[...]).astype(BF16)


def _mem_kv(mem, gain, wk, wv):
    b, m, d = mem.shape
    blk = pl.BlockSpec((None, m, d), lambda bi: (bi, 0, 0))
    out = jax.ShapeDtypeStruct((b, m, d), BF16)
    return pl.pallas_call(
        _memkv_kernel,
        grid=(b,),
        in_specs=[blk, _const_spec(gain.shape), _const_spec(wk.shape), _const_spec(wv.shape)],
        out_specs=(blk, blk),
        out_shape=(out, out),
        compiler_params=_params(1),
        name="mem_kv",
    )(mem, gain, wk, wv)


def _xattn_kernel(x_ref, gpre_ref, wq_ref, k_ref, v_ref, wo_ref, gpost_ref, out_ref, *, heads):
    x = x_ref[...]
    d = x.shape[1]
    hd = d // heads
    h = _rms(x, gpre_ref[...]).astype(BF16)
    q = (_dot(h, wq_ref[...]) * (hd ** -0.5)).astype(BF16)
    outs = []
    for i in range(heads):
        sl = slice(i * hd, (i + 1) * hd)
        s = _dot_nt(q[:, sl], k_ref[:, sl])
        p = jnp.exp(s - jnp.max(s, axis=-1, keepdims=True))
        inv = 1.0 / jnp.sum(p, axis=-1, keepdims=True)
        outs.append((_dot(p.astype(BF16), v_ref[:, sl]) * inv).astype(BF16))
    y = _dot(jnp.concatenate(outs, axis=1), wo_ref[...])
    out_ref[...] = x + _rms(y, gpost_ref[...])


def _xattn(x, gpre, wq, kx, vx, wo, gpost, *, ts):
    b, s, d = x.shape
    m = kx.shape[1]
    tok = pl.BlockSpec((None, ts, d), lambda bi, ti: (bi, ti, 0))
    memb = pl.BlockSpec((None, m, d), lambda bi, ti: (bi, 0, 0))
    return pl.pallas_call(
        functools.partial(_xattn_kernel, heads=XA_HEADS),
        grid=(b, s // ts),
        in_specs=[tok, _const_spec(gpre.shape), _const_spec(wq.shape), memb, memb,
                  _const_spec(wo.shape), _const_spec(gpost.shape)],
        out_specs=tok,
        out_shape=jax.ShapeDtypeStruct((b, s, d), F32),
        compiler_params=_params(2),
        name="xattn",
    )(x, gpre, wq, kx, vx, wo, gpost)


def _ffn_kernel(x_ref, gpre_ref, wup_ref, cw_ref, cb_ref, wdn_ref, gpost_ref, out_ref,
                carry_ref, acc_ref, *, d_ff, cwid):
    ts = x_ref.shape[0]
    halo = SUBLANES

    @pl.when(pl.program_id(1) == 0)
    def _():
        carry_ref[...] = jnp.zeros_like(carry_ref)

    x = x_ref[...]
    h = _rms(x, gpre_ref[...]).astype(BF16)

    def conv(c0):
        up = _dot(h, wup_ref[:, c0:c0 + cwid])
        ext = jnp.concatenate([carry_ref[:, c0:c0 + cwid], up], axis=0)
        carry_ref[:, c0:c0 + cwid] = up[ts - halo:, :]
        out = cb_ref[:, c0:c0 + cwid]
        for j in range(CONV_WIDTH):
            off = halo - (CONV_WIDTH - 1) + j
            out = out + cw_ref[j:j + 1, c0:c0 + cwid] * ext[off:off + ts, :]
        return out

    for c in range(d_ff // cwid):
        c0 = c * cwid
        a = jax.nn.gelu(conv(c0), approximate=True) * conv(d_ff + c0)
        part = _dot(a.astype(BF16), wdn_ref[c0:c0 + cwid, :])
        if c == 0:
            acc_ref[...] = part
        else:
            acc_ref[...] += part
    out_ref[...] = x + _rms(acc_ref[...], gpost_ref[...])


def _conv_ffn(x, gpre, w_up, conv_w, conv_b, w_dn, gpost, *, ts, cwid):
    b, s, d = x.shape
    d_ff = w_dn.shape[0]
    tok = pl.BlockSpec((None, ts, d), lambda bi, ti: (bi, ti, 0))
    return pl.pallas_call(
        functools.partial(_ffn_kernel, d_ff=d_ff, cwid=cwid),
        grid=(b, s // ts),
        in_specs=[tok, _const_spec(gpre.shape), _const_spec(w_up.shape), _const_spec(conv_w.shape),
                  _const_spec(conv_b.shape), _const_spec(w_dn.shape), _const_spec(gpost.shape)],
        out_specs=tok,
        out_shape=jax.ShapeDtypeStruct((b, s, d), F32),
        scratch_shapes=[pltpu.VMEM((SUBLANES, 2 * d_ff), F32), pltpu.VMEM((ts, d), F32)],
        compiler_params=_params(2),
        name="conv_ffn",
    )(x, gpre, w_up, conv_w, conv_b, w_dn, gpost)


def _ssm_slabs(bb_re, bb_im, c_re, c_im):
    g, p, c = bb_re.shape
    gps = LANES // c
    n_slab = g // gps
    eye = jnp.eye(gps, dtype=F32)

    def b_part(bb):
        t = bb.reshape(n_slab, gps, p, c)
        return jnp.einsum('sgpc,gh->sgchp', t, eye).reshape(n_slab, gps * c, gps * p)

    def c_part(cc):
        t = cc.reshape(n_slab, gps, c, p)
        return jnp.einsum('sgcp,gh->shpgc', t, eye).reshape(n_slab, gps * p, gps * c)

    b_slab = jnp.concatenate([b_part(bb_re), b_part(bb_im)], axis=2).astype(BF16)
    c_slab = jnp.concatenate([c_part(c_re), c_part(-c_im)], axis=1).astype(BF16)
    return b_slab, c_slab


def _layer(x, mem, l, p):
    b, s, d = x.shape
    sbw = p["w_branch_attn"].shape[1]
    ssw = p["w_branch_ssm"].shape[1]
    row = lambda a: a[l][None, :].astype(F32)
    wb = lambda a: a[l].astype(BF16)

    lam_re, lam_im, bb_re, bb_im = _discretize(
        p["ssm_a_re"][l], p["ssm_a_im"][l], p["ssm_log_dt"][l], p["ssm_b_re"][l], p["ssm_b_im"][l])
    b_slab, c_slab = _ssm_slabs(bb_re, bb_im, p["ssm_c_re"][l].astype(F32), p["ssm_c_im"][l].astype(F32))
    n_state = lam_re.size
    lam_re_b = jnp.broadcast_to(lam_re.reshape(1, n_state), (SUBLANES, n_state))
    lam_im_b = jnp.broadcast_to(lam_im.reshape(1, n_state), (SUBLANES, n_state))

    q, k, v, u_tm, ga, gs = _in_proj(x, row(p["norm_mix_pre"]), wb(p["w_in"]), row(p["b_gate"]),
                                     sbw=sbw, ssw=ssw, ts=512)
    o_attn = _sb_attn(q, k, v, tq=128)
    o_ssm = _ssm(u_tm.reshape(s * b, ssw), b_slab, lam_re_b, lam_im_b, c_slab,
                 row(p["ssm_d"]), wb(p["ssm_w_glu"]), row(p["ssm_b_glu"]), batch=b, t_steps=64)
    x = _merge_out(o_attn, o_ssm.reshape(s, b * ssw), ga, gs, x, wb(p["w_branch_attn"]),
                   wb(p["w_branch_ssm"]), wb(p["w_out"]), row(p["norm_mix_post"]), ts=512)

    kx, vx = _mem_kv(mem, row(p["norm_mem"]), wb(p["xa_wk"]), wb(p["xa_wv"]))
    x = _xattn(x, row(p["norm_xa_pre"]), wb(p["xa_wq"]), kx, vx, wb(p["xa_wo"]),
               row(p["norm_xa_post"]), ts=512)

    x = _conv_ffn(x, row(p["norm_ffn_pre"]), wb(p["ffn_w_up"]), p["ffn_conv_w"][l].astype(F32),
                  row(p["ffn_conv_b"]), wb(p["ffn_w_down"]), row(p["norm_ffn_post"]), ts=512, cwid=256)
    return x


def kernel(x, mem, norm_mix_pre, norm_mix_post, w_in, b_gate, ssm_a_re, ssm_a_im, ssm_log_dt, ssm_b_re, ssm_b_im, ssm_c_re, ssm_c_im, ssm_d, ssm_w_glu, ssm_b_glu, w_branch_attn, w_branch_ssm, w_out, norm_xa_pre, norm_xa_post, norm_mem, xa_wq, xa_wk, xa_wv, xa_wo, norm_ffn_pre, norm_ffn_post, ffn_w_up, ffn_conv_w, ffn_conv_b, ffn_w_down):
    p = dict(norm_mix_pre=norm_mix_pre, norm_mix_post=norm_mix_post, w_in=w_in, b_gate=b_gate,
             ssm_a_re=ssm_a_re, ssm_a_im=ssm_a_im, ssm_log_dt=ssm_log_dt, ssm_b_re=ssm_b_re,
             ssm_b_im=ssm_b_im, ssm_c_re=ssm_c_re, ssm_c_im=ssm_c_im, ssm_d=ssm_d,
             ssm_w_glu=ssm_w_glu, ssm_b_glu=ssm_b_glu, w_branch_attn=w_branch_attn,
             w_branch_ssm=w_branch_ssm, w_out=w_out, norm_xa_pre=norm_xa_pre,
             norm_xa_post=norm_xa_post, norm_mem=norm_mem, xa_wq=xa_wq, xa_wk=xa_wk, xa_wv=xa_wv,
             xa_wo=xa_wo, norm_ffn_pre=norm_ffn_pre, norm_ffn_post=norm_ffn_post,
             ffn_w_up=ffn_w_up, ffn_conv_w=ffn_conv_w, ffn_conv_b=ffn_conv_b, ffn_w_down=ffn_w_down)
    for l in range(w_in.shape[0]):
        x = _layer(x, mem, l, p)
    return x
```

```python
import functools
import math

import numpy as np

import jax
import jax.numpy as jnp
from jax import lax
from jax.experimental import pallas as pl
from jax.experimental.pallas import tpu as pltpu

F32 = jnp.float32
BF16 = jnp.bfloat16

RMS_EPS = 1e-6
SB_HEAD_DIM = 64
SSM_GROUP = 16
SSM_STATE = 64
XA_HEADS = 4
CONV_WIDTH = 3
LANES = 128
SUBLANES = 8
VMEM_LIMIT = 56 * 1024 * 1024


def _rms(x, g):
    ms = jnp.mean(x * x, axis=-1, keepdims=True)
    return x * lax.rsqrt(ms + RMS_EPS) * g


def _dot(a, b):
    return jnp.dot(a, b, preferred_element_type=F32)


def _dot_nt(a, b):
    return lax.dot_general(a, b, (((1,), (1,)), ((), ())), preferred_element_type=F32)


def _params(n_axes):
    return pltpu.CompilerParams(
        dimension_semantics=("arbitrary",) * n_axes, vmem_limit_bytes=VMEM_LIMIT)


def _const_spec(shape):
    zeros = (0,) * len(shape)
    return pl.BlockSpec(shape, lambda *_: zeros, pipeline_mode=pl.Buffered(1))


def _discretize_kernel(are_ref, aim_ref, ldt_ref, bre_ref, bim_ref,
                       lre_ref, lim_ref, bbre_ref, bbim_ref):
    a_re = are_ref[...]
    a_im = aim_ref[...]
    dt = jnp.exp(ldt_ref[...])
    mag = jnp.exp(a_re * dt)
    l_re = mag * jnp.cos(a_im * dt)
    l_im = mag * jnp.sin(a_im * dt)
    lre_ref[...] = l_re
    lim_ref[...] = l_im
    n_re = l_re - 1.0
    inv = 1.0 / (a_re * a_re + a_im * a_im)
    c_re = (n_re * a_re + l_im * a_im) * inv
    c_im = (l_im * a_re - n_re * a_im) * inv
    b_re = bre_ref[...]
    b_im = bim_ref[...]
    bbre_ref[...] = c_re * b_re - c_im * b_im
    bbim_ref[...] = c_re * b_im + c_im * b_re


def _discretize(a_re, a_im, log_dt, b_re, b_im):
    g, p, c = b_re.shape
    rep = lambda a: jnp.repeat(a, c, axis=-1)
    args = (rep(a_re), rep(a_im), jnp.broadcast_to(log_dt[:, None], (g, p * c)),
            b_re.reshape(g, p * c), b_im.reshape(g, p * c))
    out = jax.ShapeDtypeStruct((g, p * c), F32)
    l_re, l_im, bb_re, bb_im = pl.pallas_call(
        _discretize_kernel, out_shape=(out, out, out, out), name="ssm_discretize")(*args)
    return (l_re[:, ::c], l_im[:, ::c], bb_re.reshape(g, p, c), bb_im.reshape(g, p, c))


def _inproj_kernel(x_ref, g_ref, w_ref, bg_ref, qt_ref, k_ref, vt_ref, u_ref, ga_ref, gs_ref,
                   *, sbw, ssw, d, qscale, tq):
    ts = x_ref.shape[0]
    h = _rms(x_ref[...], g_ref[...]).astype(BF16)

    def proj(c0, n):
        return _dot(h, w_ref[:, c0:c0 + n])

    q = proj(0, sbw) * qscale
    v = proj(2 * sbw, sbw)
    for hp in range(sbw // LANES):
        cols = slice(hp * LANES, (hp + 1) * LANES)
        for c in range(ts // tq):
            rows = slice(c * tq, (c + 1) * tq)
            qt_ref[hp, c] = q[rows, cols].T.astype(BF16)
            vt_ref[hp, c] = v[rows, cols].T.astype(BF16)
    k_ref[...] = proj(sbw, sbw).astype(BF16)
    u_ref[...] = proj(3 * sbw, ssw).astype(BF16)
    g0 = 3 * sbw + ssw
    cw = 512
    for c in range(d // cw):
        sl = slice(c * cw, (c + 1) * cw)
        ga_ref[:, sl] = jax.nn.sigmoid(proj(g0 + c * cw, cw) + bg_ref[:, sl]).astype(BF16)
        gs_ref[:, sl] = jax.nn.sigmoid(
            proj(g0 + d + c * cw, cw) + bg_ref[:, d + c * cw:d + (c + 1) * cw]).astype(BF16)


def _in_proj(x, gain, w_in, b_gate, *, sbw, ssw, ts, tq):
    b, s, d = x.shape
    in_w = w_in.shape[1]
    hp = sbw // LANES
    qscale = SB_HEAD_DIM ** -0.5
    tok = lambda w: pl.BlockSpec((None, ts, w), lambda bi, ti: (bi, ti, 0))
    tr = pl.BlockSpec((None, hp, ts // tq, LANES, tq), lambda bi, ti: (bi, 0, ti, 0, 0))
    out_shape = (
        jax.ShapeDtypeStruct((b, hp, s // tq, LANES, tq), BF16),
        jax.ShapeDtypeStruct((b, s, sbw), BF16),
        jax.ShapeDtypeStruct((b, hp, s // tq, LANES, tq), BF16),
        jax.ShapeDtypeStruct((b, s, ssw), BF16),
        jax.ShapeDtypeStruct((b, s, d), BF16),
        jax.ShapeDtypeStruct((b, s, d), BF16),
    )
    return pl.pallas_call(
        functools.partial(_inproj_kernel, sbw=sbw, ssw=ssw, d=d, qscale=qscale, tq=tq),
        grid=(b, s // ts),
        in_specs=[tok(d), _const_spec((1, d)), _const_spec((d, in_w)), _const_spec((1, 2 * d))],
        out_specs=(tr, tok(sbw), tr, tok(ssw), tok(d), tok(d)),
        out_shape=out_shape,
        compiler_params=_params(2),
        name="in_proj",
    )(x, gain, w_in, b_gate)


_SB_DEPTH = 5
_SB_Z_SLOTS = 4
_SB_SLOTS = 2
_SB_MASK_BIAS = -1e30


def _suffix_matrix(nk):
    r = lax.broadcasted_iota(jnp.int32, (nk + SUBLANES, nk), 0)
    c = lax.broadcasted_iota(jnp.int32, (nk + SUBLANES, nk), 1)
    return ((c >= r) | (r >= nk)).astype(BF16)


def _sb_item_table(nq):
    items = [(qi, qi - j, int(j == 0), int(j == qi)) for qi in range(nq) for j in range(qi + 1)]
    n = len(items)
    n_steps = -(-(n + _SB_DEPTH - 1) // _SB_Z_SLOTS) * _SB_Z_SLOTS
    tab = np.zeros((8, n_steps), np.int32)
    for t in range(n_steps):
        qi, c, first, _ = items[min(t, n - 1)]
        tab[0:3, t] = (qi, c, first if t < n else 0)
        if 0 <= t - 3 < n:
            tab[3, t] = items[t - 3][2]
        if 0 <= t - 4 < n:
            tab[4:8, t] = items[t - 4]
    return tab


def _sb_attn_kernel(tab_ref, qt_ref, k_ref, vt_ref, u_ref, o_ref,
                    z_ref, sp_ref, ct_ref, w_ref, acc_ref, r_ref, *, tq, n_steps):
    drow = lax.broadcasted_iota(jnp.int32, (LANES, tq), 0)
    head_rows = [(drow // SB_HEAD_DIM) == h for h in range(2)]
    sign = jnp.uint32(0x80000000)

    for ref in (z_ref, sp_ref, ct_ref, w_ref, acc_ref, r_ref):
        ref[...] = jnp.zeros_like(ref)

    def m1(t, zs):
        qt = qt_ref[tab_ref[0, t]]
        ks = pl.multiple_of(tab_ref[1, t] * tq, tq)
        kc = k_ref[pl.ds(ks, tq), :]
        for h in range(2):
            z_ref[zs, h] = _dot(kc, jnp.where(head_rows[h], qt, jnp.zeros_like(qt)))

    def mask_diagonal(t, zs):
        @pl.when(tab_ref[2, t] > 0)
        def _():
            krow = lax.broadcasted_iota(jnp.int32, (tq, tq), 0)
            qcol = lax.broadcasted_iota(jnp.int32, (tq, tq), 1)
            bias = jnp.where(krow < qcol, 0.0, _SB_MASK_BIAS)
            for h in range(2):
                z_ref[zs, h] = z_ref[zs, h] + bias

    def v1(zs, s):
        for h in range(2):
            z = z_ref[zs, h]
            neg_abs = lax.bitcast_convert_type(lax.bitcast_convert_type(z, jnp.uint32) | sign, F32)
            sp = jnp.maximum(z, 0.0) + jnp.log(1.0 + jnp.exp(neg_abs))
            sp_ref[s, h] = sp.astype(BF16)

    def m2(s_in, s_out):
        for h in range(2):
            ct_ref[s_out, h] = _dot(u_ref[...], sp_ref[s_in, h])

    def v2(t, zs, s_in, s_out):
        first = tab_ref[3, t] > 0
        for h in range(2):
            r = jnp.where(first, 0.0, r_ref[h])
            w = jnp.exp(z_ref[zs, h] - ct_ref[s_in, h, :tq, :] - r[0:1, :])
            w_ref[s_out, h] = w.astype(BF16)
            r_ref[h] = r + ct_ref[s_in, h, tq:, :]

    def m3(t, s_in):
        first = tab_ref[6, t] > 0
        vc = vt_ref[tab_ref[5, t]]
        for h in range(2):
            pv = _dot(vc[h * SB_HEAD_DIM:(h + 1) * SB_HEAD_DIM, :], w_ref[s_in, h])
            acc_ref[h] = jnp.where(first, 0.0, acc_ref[h]) + pv

    def emit(t):
        @pl.when(tab_ref[7, t] > 0)
        def _():
            rows = pl.ds(pl.multiple_of(tab_ref[4, t] * tq, tq), tq)
            acc = jnp.concatenate([acc_ref[0], acc_ref[1]], axis=0)
            o_ref[rows, :] = acc.T.astype(o_ref.dtype)

    def outer(i, carry):
        for u in range(_SB_Z_SLOTS):
            t = i * _SB_Z_SLOTS + u
            cur, prev = u % _SB_SLOTS, (u - 1) % _SB_SLOTS
            m3(t, prev)
            v2(t, (u - 3) % _SB_Z_SLOTS, prev, cur)
            m2(prev, cur)
            v1((u - 1) % _SB_Z_SLOTS, cur)
            m1(t, u)
            mask_diagonal(t, u)
            emit(t)
        return carry

    lax.fori_loop(0, n_steps // _SB_Z_SLOTS, outer, 0)


def _sb_attn(qt, k, vt, *, tq):
    b, hp, nq, _, _ = qt.shape
    s = k.shape[1]
    tab = _sb_item_table(nq)
    n_steps = tab.shape[1]
    u_mat = _suffix_matrix(tq)
    tr = pl.BlockSpec((None, None, nq, LANES, tq), lambda bi, hi, tab: (bi, hi, 0, 0, 0))
    tokb = pl.BlockSpec((None, s, LANES), lambda bi, hi, tab: (bi, 0, hi))
    grid_spec = pltpu.PrefetchScalarGridSpec(
        num_scalar_prefetch=1,
        grid=(b, hp),
        in_specs=[tr, tokb, tr,
                  pl.BlockSpec(u_mat.shape, lambda bi, hi, tab: (0, 0), pipeline_mode=pl.Buffered(1))],
        out_specs=tokb,
        scratch_shapes=[
            pltpu.VMEM((_SB_Z_SLOTS, 2, tq, tq), F32),
            pltpu.VMEM((_SB_SLOTS, 2, tq, tq), BF16),
            pltpu.VMEM((_SB_SLOTS, 2, tq + SUBLANES, tq), F32),
            pltpu.VMEM((_SB_SLOTS, 2, tq, tq), BF16),
            pltpu.VMEM((2, SB_HEAD_DIM, tq), F32),
            pltpu.VMEM((2, SUBLANES, tq), F32),
        ],
    )
    return pl.pallas_call(
        functools.partial(_sb_attn_kernel, tq=tq, n_steps=n_steps),
        grid_spec=grid_spec,
        out_shape=jax.ShapeDtypeStruct((b, s, hp * LANES), BF16),
        compiler_params=_params(2),
        name="sb_attn",
    )(jnp.asarray(tab), qt, k, vt, u_mat)


def _ssm_kernel(u_ref, bs_ref, are_ref, aim_ref, cs_ref, d_ref, wg_ref, bg_ref, o_ref,
                bu_ref, xs_ref, *, t_steps, n_state):
    nb = u_ref.shape[0]
    n_slab = u_ref.shape[2] // LANES
    sw = n_state // n_slab

    @pl.when(pl.program_id(0) == 0)
    def _():
        xs_ref[...] = jnp.zeros_like(xs_ref)

    tps = sw // LANES
    n_im0 = n_state // LANES
    ub = u_ref[...].reshape(nb * t_steps, u_ref.shape[2])
    for s in range(n_slab):
        bu = _dot(ub[:, s * LANES:(s + 1) * LANES], bs_ref[s])
        for j in range(tps):
            bu_ref[s * tps + j] = bu[:, j * LANES:(j + 1) * LANES]
            bu_ref[n_im0 + s * tps + j] = bu[:, sw + j * LANES:sw + (j + 1) * LANES]

    n_part = 2
    tpp = n_im0 // n_part
    for part in range(n_part):
        tiles = [part * tpp + j for j in range(tpp)]
        lane = lambda j: slice(j * LANES, (j + 1) * LANES)
        a_re = [are_ref[:, lane(j)] for j in tiles]
        a_im = [aim_ref[:, lane(j)] for j in tiles]

        def body(t, carry, tiles=tiles, a_re=a_re, a_im=a_im):
            x_re, x_im = carry
            rows = pl.ds(t, nb, stride=t_steps)
            n_re, n_im = [], []
            for i, j in enumerate(tiles):
                v_re = a_re[i] * x_re[i] - a_im[i] * x_im[i] + bu_ref[j, rows, :]
                v_im = a_re[i] * x_im[i] + a_im[i] * x_re[i] + bu_ref[n_im0 + j, rows, :]
                bu_ref[j, rows, :] = v_re
                bu_ref[n_im0 + j, rows, :] = v_im
                n_re.append(v_re)
                n_im.append(v_im)
            return tuple(n_re), tuple(n_im)

        init = (tuple(xs_ref[:, lane(j)] for j in tiles),
                tuple(xs_ref[:, lane(n_im0 + j)] for j in tiles))
        x_re, x_im = lax.fori_loop(0, t_steps, body, init, unroll=2)
        for i, j in enumerate(tiles):
            xs_ref[:, lane(j)] = x_re[i]
            xs_ref[:, lane(n_im0 + j)] = x_im[i]

    ys = []
    for s in range(n_slab):
        x_re = jnp.concatenate([bu_ref[s * tps + j] for j in range(tps)], axis=1).astype(BF16)
        x_im = jnp.concatenate([bu_ref[n_im0 + s * tps + j] for j in range(tps)], axis=1).astype(BF16)
        ys.append(_dot(x_re, cs_ref[s, :sw, :]) + _dot(x_im, cs_ref[s, sw:, :]))
    y = jnp.concatenate(ys, axis=1) + d_ref[...] * ub.astype(F32)
    y = jax.nn.gelu(y, approximate=True)
    gate = jax.nn.sigmoid(_dot(y.astype(BF16), wg_ref[...]) + bg_ref[...])
    o_ref[...] = (y * gate).astype(o_ref.dtype).reshape(o_ref.shape)


def _ssm(u, b_slab, lam_re, lam_im, c_slab, d_skip, w_glu, b_glu, *, t_steps):
    batch, s, ssw = u.shape
    n_state = lam_re.shape[1]
    assert batch == SUBLANES, "the scan puts the batch on the sublanes"
    blk = t_steps * batch
    tblk = pl.BlockSpec((batch, t_steps, ssw), lambda i: (0, i, 0))
    return pl.pallas_call(
        functools.partial(_ssm_kernel, t_steps=t_steps, n_state=n_state),
        grid=(s // t_steps,),
        in_specs=[
            tblk,
            _const_spec(b_slab.shape), _const_spec(lam_re.shape), _const_spec(lam_im.shape),
            _const_spec(c_slab.shape), _const_spec(d_skip.shape), _const_spec(w_glu.shape),
            _const_spec(b_glu.shape),
        ],
        out_specs=tblk,
        out_shape=jax.ShapeDtypeStruct((batch, s, ssw), BF16),
        scratch_shapes=[pltpu.VMEM((2 * n_state // LANES, blk, LANES), F32),
                        pltpu.VMEM((batch, 2 * n_state), F32)],
        compiler_params=_params(1),
        name="ssm",
    )(u, b_slab, lam_re, lam_im, c_slab, d_skip, w_glu, b_glu)


def _merge_kernel(oa_ref, os_ref, ga_ref, gs_ref, x_ref, wa_ref, ws_ref, wo_ref, g_ref, out_ref):
    pa = _dot(oa_ref[...], wa_ref[...])
    ps = _dot(os_ref[...], ws_ref[...])
    merged = ga_ref[...].astype(F32) * pa + gs_ref[...].astype(F32) * ps
    y = _dot(merged.astype(BF16), wo_ref[...])
    out_ref[...] = x_ref[...] + _rms(y, g_ref[...])


def _merge_out(o_attn, o_ssm, ga, gs, x, wa, ws, wo, gain, *, ts):
    b, s, d = x.shape
    sbw = o_attn.shape[2]
    ssw = o_ssm.shape[2]
    tok = lambda w: pl.BlockSpec((None, ts, w), lambda bi, ti: (bi, ti, 0))
    return pl.pallas_call(
        _merge_kernel,
        grid=(b, s // ts),
        in_specs=[tok(sbw), tok(ssw), tok(d), tok(d), tok(d),
                  _const_spec(wa.shape), _const_spec(ws.shape), _const_spec(wo.shape),
                  _const_spec(gain.shape)],
        out_specs=tok(d),
        out_shape=jax.ShapeDtypeStruct((b, s, d), F32),
        compiler_params=_params(2),
        name="merge_out",
    )(o_attn, o_ssm, ga, gs, x, wa, ws, wo, gain)


def _memkv_kernel(m_ref, g_ref, wk_ref, wv_ref, k_ref, v_ref):
    mn = _rms(m_ref[...], g_ref[...]).astype(BF16)
    k_ref[...] = _dot(mn, wk_ref[...]).astype(BF16)
    v_ref[...] = _dot(mn, wv_ref[...]).astype(BF16)


def _mem_kv(mem, gain, wk, wv):
    b, m, d = mem.shape
    blk = pl.BlockSpec((None, m, d), lambda bi: (bi, 0, 0))
    out = jax.ShapeDtypeStruct((b, m, d), BF16)
    return pl.pallas_call(
        _memkv_kernel,
        grid=(b,),
        in_specs=[blk, _const_spec(gain.shape), _const_spec(wk.shape), _const_spec(wv.shape)],
        out_specs=(blk, blk),
        out_shape=(out, out),
        compiler_params=_params(1),
        name="mem_kv",
    )(mem, gain, wk, wv)


def _xattn_kernel(x_ref, gpre_ref, wq_ref, k_ref, v_ref, wo_ref, gpost_ref, out_ref, *, heads):
    x = x_ref[...]
    d = x.shape[1]
    hd = d // heads
    h = _rms(x, gpre_ref[...]).astype(BF16)
    q = (_dot(h, wq_ref[...]) * (hd ** -0.5)).astype(BF16)
    outs = []
    for i in range(heads):
        sl = slice(i * hd, (i + 1) * hd)
        s = _dot_nt(q[:, sl], k_ref[:, sl])
        p = jnp.exp(s - jnp.max(s, axis=-1, keepdims=True))
        inv = 1.0 / jnp.sum(p, axis=-1, keepdims=True)
        outs.append((_dot(p.astype(BF16), v_ref[:, sl]) * inv).astype(BF16))
    y = _dot(jnp.concatenate(outs, axis=1), wo_ref[...])
    out_ref[...] = x + _rms(y, gpost_ref[...])


def _xattn(x, gpre, wq, kx, vx, wo, gpost, *, ts):
    b, s, d = x.shape
    m = kx.shape[1]
    tok = pl.BlockSpec((None, ts, d), lambda bi, ti: (bi, ti, 0))
    memb = pl.BlockSpec((None, m, d), lambda bi, ti: (bi, 0, 0))
    return pl.pallas_call(
        functools.partial(_xattn_kernel, heads=XA_HEADS),
        grid=(b, s // ts),
        in_specs=[tok, _const_spec(gpre.shape), _const_spec(wq.shape), memb, memb,
                  _const_spec(wo.shape), _const_spec(gpost.shape)],
        out_specs=tok,
        out_shape=jax.ShapeDtypeStruct((b, s, d), F32),
        compiler_params=_params(2),
        name="xattn",
    )(x, gpre, wq, kx, vx, wo, gpost)


def _ffn_kernel(x_ref, gpre_ref, wup_ref, cw_ref, cb_ref, wdn_ref, gpost_ref, out_ref,
                carry_ref, acc_ref, *, d_ff, cwid):
    ts = x_ref.shape[0]
    halo = SUBLANES

    @pl.when(pl.program_id(1) == 0)
    def _():
        carry_ref[...] = jnp.zeros_like(carry_ref)

    x = x_ref[...]
    h = _rms(x, gpre_ref[...]).astype(BF16)

    def conv(c0):
        up = _dot(h, wup_ref[:, c0:c0 + cwid])
        ext = jnp.concatenate([carry_ref[:, c0:c0 + cwid], up], axis=0)
        carry_ref[:, c0:c0 + cwid] = up[ts - halo:, :]
        out = cb_ref[:, c0:c0 + cwid]
        for j in range(CONV_WIDTH):
            off = halo - (CONV_WIDTH - 1) + j
            out = out + cw_ref[j:j + 1, c0:c0 + cwid] * ext[off:off + ts, :]
        return out

    for c in range(d_ff // cwid):
        c0 = c * cwid
        a = jax.nn.gelu(conv(c0), approximate=True) * conv(d_ff + c0)
        part = _dot(a.astype(BF16), wdn_ref[c0:c0 + cwid, :])
        if c == 0:
            acc_ref[...] = part
        else:
            acc_ref[...] += part
    out_ref[...] = x + _rms(acc_ref[...], gpost_ref[...])


def _conv_ffn(x, gpre, w_up, conv_w, conv_b, w_dn, gpost, *, ts, cwid):
    b, s, d = x.shape
    d_ff = w_dn.shape[0]
    tok = pl.BlockSpec((None, ts, d), lambda bi, ti: (bi, ti, 0))
    return pl.pallas_call(
        functools.partial(_ffn_kernel, d_ff=d_ff, cwid=cwid),
        grid=(b, s // ts),
        in_specs=[tok, _const_spec(gpre.shape), _const_spec(w_up.shape), _const_spec(conv_w.shape),
                  _const_spec(conv_b.shape), _const_spec(w_dn.shape), _const_spec(gpost.shape)],
        out_specs=tok,
        out_shape=jax.ShapeDtypeStruct((b, s, d), F32),
        scratch_shapes=[pltpu.VMEM((SUBLANES, 2 * d_ff), F32), pltpu.VMEM((ts, d), F32)],
        compiler_params=_params(2),
        name="conv_ffn",
    )(x, gpre, w_up, conv_w, conv_b, w_dn, gpost)


def _ssm_slabs(bb_re, bb_im, c_re, c_im):
    g, p, c = bb_re.shape
    gps = LANES // c
    n_slab = g // gps
    eye = jnp.eye(gps, dtype=F32)

    def b_part(bb):
        t = bb.reshape(n_slab, gps, p, c)
        return jnp.einsum('sgpc,gh->sgchp', t, eye).reshape(n_slab, gps * c, gps * p)

    def c_part(cc):
        t = cc.reshape(n_slab, gps, c, p)
        return jnp.einsum('sgcp,gh->shpgc', t, eye).reshape(n_slab, gps * p, gps * c)

    b_slab = jnp.concatenate([b_part(bb_re), b_part(bb_im)], axis=2).astype(BF16)
    c_slab = jnp.concatenate([c_part(c_re), c_part(-c_im)], axis=1).astype(BF16)
    return b_slab, c_slab


def _layer(x, mem, l, p):
    b, s, d = x.shape
    sbw = p["w_branch_attn"].shape[1]
    ssw = p["w_branch_ssm"].shape[1]
    row = lambda a: a[l][None, :].astype(F32)
    wb = lambda a: a[l].astype(BF16)

    lam_re, lam_im, bb_re, bb_im = _discretize(
        p["ssm_a_re"][l], p["ssm_a_im"][l], p["ssm_log_dt"][l], p["ssm_b_re"][l], p["ssm_b_im"][l])
    b_slab, c_slab = _ssm_slabs(bb_re, bb_im, p["ssm_c_re"][l].astype(F32), p["ssm_c_im"][l].astype(F32))
    n_state = lam_re.size
    lam_re_b = jnp.broadcast_to(lam_re.reshape(1, n_state), (SUBLANES, n_state))
    lam_im_b = jnp.broadcast_to(lam_im.reshape(1, n_state), (SUBLANES, n_state))

    qt, k, vt, u, ga, gs = _in_proj(x, row(p["norm_mix_pre"]), wb(p["w_in"]), row(p["b_gate"]),
                                       sbw=sbw, ssw=ssw, ts=512, tq=256)
    o_attn = _sb_attn(qt, k, vt, tq=256)
    o_ssm = _ssm(u, b_slab, lam_re_b, lam_im_b, c_slab,
                 row(p["ssm_d"]), wb(p["ssm_w_glu"]), row(p["ssm_b_glu"]), t_steps=64)
    x = _merge_out(o_attn, o_ssm, ga, gs, x, wb(p["w_branch_attn"]),
                   wb(p["w_branch_ssm"]), wb(p["w_out"]), row(p["norm_mix_post"]), ts=512)

    kx, vx = _mem_kv(mem, row(p["norm_mem"]), wb(p["xa_wk"]), wb(p["xa_wv"]))
    x = _xattn(x, row(p["norm_xa_pre"]), wb(p["xa_wq"]), kx, vx, wb(p["xa_wo"]),
               row(p["norm_xa_post"]), ts=512)

    x = _conv_ffn(x, row(p["norm_ffn_pre"]), wb(p["ffn_w_up"]), p["ffn_conv_w"][l].astype(F32),
                  row(p["ffn_conv_b"]), wb(p["ffn_w_down"]), row(p["norm_ffn_post"]), ts=512, cwid=256)
    return x


def kernel(x, mem, norm_mix_pre, norm_mix_post, w_in, b_gate, ssm_a_re, ssm_a_im, ssm_log_dt, ssm_b_re, ssm_b_im, ssm_c_re, ssm_c_im, ssm_d, ssm_w_glu, ssm_b_glu, w_branch_attn, w_branch_ssm, w_out, norm_xa_pre, norm_xa_post, norm_mem, xa_wq, xa_wk, xa_wv, xa_wo, norm_ffn_pre, norm_ffn_post, ffn_w_up, ffn_conv_w, ffn_conv_b, ffn_w_down):
    p = dict(norm_mix_pre=norm_mix_pre, norm_mix_post=norm_mix_post, w_in=w_in, b_gate=b_gate,
             ssm_a_re=ssm_a_re, ssm_a_im=ssm_a_im, ssm_log_dt=ssm_log_dt, ssm_b_re=ssm_b_re,
             ssm_b_im=ssm_b_im, ssm_c_re=ssm_c_re, ssm_c_im=ssm_c_im, ssm_d=ssm_d,
             ssm_w_glu=ssm_w_glu, ssm_b_glu=ssm_b_glu, w_branch_attn=w_branch_attn,
             w_branch_ssm=w_branch_ssm, w_out=w_out, norm_xa_pre=norm_xa_pre,
             norm_xa_post=norm_xa_post, norm_mem=norm_mem, xa_wq=xa_wq, xa_wk=xa_wk, xa_wv=xa_wv,
             xa_wo=xa_wo, norm_ffn_pre=norm_ffn_pre, norm_ffn_post=norm_ffn_post,
             ffn_w_up=ffn_w_up, ffn_conv_w=ffn_conv_w, ffn_conv_b=ffn_conv_b, ffn_w_down=ffn_w_down)
    for l in range(w_in.shape[0]):
        x = _layer(x, mem, l, p)
    return x
```

```python
import functools
import math

import numpy as np

import jax
import jax.numpy as jnp
from jax import lax
from jax.experimental import pallas as pl
from jax.experimental.pallas import tpu as pltpu

F32 = jnp.float32
BF16 = jnp.bfloat16

RMS_EPS = 1e-6
SB_HEAD_DIM = 64
SSM_GROUP = 16
SSM_STATE = 64
XA_HEADS = 4
CONV_WIDTH = 3
LANES = 128
SUBLANES = 8
VMEM_LIMIT = 56 * 1024 * 1024


def _rms(x, g):
    ms = jnp.mean(x * x, axis=-1, keepdims=True)
    return x * lax.rsqrt(ms + RMS_EPS) * g


def _dot(a, b):
    return jnp.dot(a, b, preferred_element_type=F32)


def _dot_nt(a, b):
    return lax.dot_general(a, b, (((1,), (1,)), ((), ())), preferred_element_type=F32)


def _params(n_axes):
    return pltpu.CompilerParams(
        dimension_semantics=("arbitrary",) * n_axes, vmem_limit_bytes=VMEM_LIMIT)


def _const_spec(shape):
    zeros = (0,) * len(shape)
    return pl.BlockSpec(shape, lambda *_: zeros, pipeline_mode=pl.Buffered(1))


def _discretize_kernel(are_ref, aim_ref, ldt_ref, bre_ref, bim_ref,
                       lre_ref, lim_ref, bbre_ref, bbim_ref):
    a_re = are_ref[...]
    a_im = aim_ref[...]
    dt = jnp.exp(ldt_ref[...])
    mag = jnp.exp(a_re * dt)
    l_re = mag * jnp.cos(a_im * dt)
    l_im = mag * jnp.sin(a_im * dt)
    lre_ref[...] = l_re
    lim_ref[...] = l_im
    n_re = l_re - 1.0
    inv = 1.0 / (a_re * a_re + a_im * a_im)
    c_re = (n_re * a_re + l_im * a_im) * inv
    c_im = (l_im * a_re - n_re * a_im) * inv
    b_re = bre_ref[...]
    b_im = bim_ref[...]
    bbre_ref[...] = c_re * b_re - c_im * b_im
    bbim_ref[...] = c_re * b_im + c_im * b_re


def _discretize(a_re, a_im, log_dt, b_re, b_im):
    g, p, c = b_re.shape
    rep = lambda a: jnp.repeat(a, c, axis=-1)
    args = (rep(a_re), rep(a_im), jnp.broadcast_to(log_dt[:, None], (g, p * c)),
            b_re.reshape(g, p * c), b_im.reshape(g, p * c))
    out = jax.ShapeDtypeStruct((g, p * c), F32)
    l_re, l_im, bb_re, bb_im = pl.pallas_call(
        _discretize_kernel, out_shape=(out, out, out, out), name="ssm_discretize")(*args)
    return (l_re[:, ::c], l_im[:, ::c], bb_re.reshape(g, p, c), bb_im.reshape(g, p, c))


def _inproj_kernel(x_ref, g_ref, w_ref, bg_ref, qt_ref, k_ref, vt_ref, u_ref, ga_ref, gs_ref,
                   *, sbw, ssw, d, qscale, tq):
    ts = x_ref.shape[0]
    h = _rms(x_ref[...], g_ref[...]).astype(BF16)

    def proj(c0, n):
        return _dot(h, w_ref[:, c0:c0 + n])

    q = proj(0, sbw) * qscale
    v = proj(2 * sbw, sbw)
    for hp in range(sbw // LANES):
        cols = slice(hp * LANES, (hp + 1) * LANES)
        for c in range(ts // tq):
            rows = slice(c * tq, (c + 1) * tq)
            qt_ref[hp, c] = q[rows, cols].T.astype(BF16)
            vt_ref[hp, c] = v[rows, cols].T.astype(BF16)
    k_ref[...] = proj(sbw, sbw).astype(BF16)
    u_ref[...] = proj(3 * sbw, ssw).astype(BF16)
    g0 = 3 * sbw + ssw
    cw = 512
    for c in range(d // cw):
        sl = slice(c * cw, (c + 1) * cw)
        ga_ref[:, sl] = jax.nn.sigmoid(proj(g0 + c * cw, cw) + bg_ref[:, sl]).astype(BF16)
        gs_ref[:, sl] = jax.nn.sigmoid(
            proj(g0 + d + c * cw, cw) + bg_ref[:, d + c * cw:d + (c + 1) * cw]).astype(BF16)


def _in_proj(x, gain, w_in, b_gate, *, sbw, ssw, ts, tq):
    b, s, d = x.shape
    in_w = w_in.shape[1]
    hp = sbw // LANES
    qscale = SB_HEAD_DIM ** -0.5
    tok = lambda w: pl.BlockSpec((None, ts, w), lambda bi, ti: (bi, ti, 0))
    tr = pl.BlockSpec((None, hp, ts // tq, LANES, tq), lambda bi, ti: (bi, 0, ti, 0, 0))
    out_shape = (
        jax.ShapeDtypeStruct((b, hp, s // tq, LANES, tq), BF16),
        jax.ShapeDtypeStruct((b, s, sbw), BF16),
        jax.ShapeDtypeStruct((b, hp, s // tq, LANES, tq), BF16),
        jax.ShapeDtypeStruct((b, s, ssw), BF16),
        jax.ShapeDtypeStruct((b, s, d), BF16),
        jax.ShapeDtypeStruct((b, s, d), BF16),
    )
    return pl.pallas_call(
        functools.partial(_inproj_kernel, sbw=sbw, ssw=ssw, d=d, qscale=qscale, tq=tq),
        grid=(b, s // ts),
        in_specs=[tok(d), _const_spec((1, d)), _const_spec((d, in_w)), _const_spec((1, 2 * d))],
        out_specs=(tr, tok(sbw), tr, tok(ssw), tok(d), tok(d)),
        out_shape=out_shape,
        compiler_params=_params(2),
        name="in_proj",
    )(x, gain, w_in, b_gate)


_SB_DEPTH = 5
_SB_Z_SLOTS = 4
_SB_SLOTS = 2
_SB_MASK_BIAS = -1e30


def _suffix_matrix(nk):
    r = lax.broadcasted_iota(jnp.int32, (nk + SUBLANES, nk), 0)
    c = lax.broadcasted_iota(jnp.int32, (nk + SUBLANES, nk), 1)
    return ((c >= r) | (r >= nk)).astype(BF16)


def _sb_item_table(nq):
    items = [(qi, qi - j, int(j == 0), int(j == qi)) for qi in range(nq) for j in range(qi + 1)]
    n = len(items)
    n_steps = -(-(n + _SB_DEPTH - 1) // _SB_Z_SLOTS) * _SB_Z_SLOTS
    tab = np.zeros((8, n_steps), np.int32)
    for t in range(n_steps):
        qi, c, first, _ = items[min(t, n - 1)]
        tab[0:3, t] = (qi, c, first if t < n else 0)
        if 0 <= t - 3 < n:
            tab[3, t] = items[t - 3][2]
        tab[4:8, t] = items[t - 4] if 0 <= t - 4 < n else (nq, 0, 0, 0)
    return tab


def _sb_attn_kernel(tab_ref, qt_ref, k_ref, vt_ref, u_ref, bias_ref, o_ref,
                    z_ref, sp_ref, ct_ref, w_ref, acc_ref, r_ref, *, tq, n_steps):
    nq = qt_ref.shape[0]
    drow = lax.broadcasted_iota(jnp.int32, (LANES, tq), 0)
    head_rows = [(drow // SB_HEAD_DIM) == h for h in range(2)]
    sign = jnp.uint32(0x80000000)

    for ref in (z_ref, sp_ref, ct_ref, w_ref, acc_ref, r_ref):
        ref[...] = jnp.zeros_like(ref)

    def m1(t, zs):
        qt = qt_ref[tab_ref[0, t]]
        ks = pl.multiple_of(tab_ref[1, t] * tq, tq)
        kc = k_ref[pl.ds(ks, tq), :]
        bias = bias_ref[tab_ref[2, t]]
        for h in range(2):
            z_ref[zs, h] = _dot(kc, jnp.where(head_rows[h], qt, jnp.zeros_like(qt))) + bias

    def v1(zs, s):
        for h in range(2):
            z = z_ref[zs, h]
            neg_abs = lax.bitcast_convert_type(lax.bitcast_convert_type(z, jnp.uint32) | sign, F32)
            sp = jnp.maximum(z, 0.0) + jnp.log(1.0 + jnp.exp(neg_abs))
            sp_ref[s, h] = sp.astype(BF16)

    def m2(s_in, s_out):
        for h in range(2):
            ct_ref[s_out, h] = _dot(u_ref[...], sp_ref[s_in, h])

    def v2(t, zs, s_in, s_out):
        first = tab_ref[3, t] > 0
        for h in range(2):
            r = jnp.where(first, 0.0, r_ref[h])
            w = jnp.exp(z_ref[zs, h] - ct_ref[s_in, h, :tq, :] - r[0:1, :])
            w_ref[s_out, h] = w.astype(BF16)
            r_ref[h] = r + ct_ref[s_in, h, tq:, :]

    def m3(t, s_in):
        qi = tab_ref[4, t]
        first = tab_ref[6, t] > 0
        vc = vt_ref[tab_ref[5, t]]
        for h in range(2):
            pv = _dot(vc[h * SB_HEAD_DIM:(h + 1) * SB_HEAD_DIM, :], w_ref[s_in, h])
            acc_ref[qi, h] = jnp.where(first, 0.0, acc_ref[qi, h]) + pv

    def outer(i, carry):
        for u in range(_SB_Z_SLOTS):
            t = i * _SB_Z_SLOTS + u
            cur, prev = u % _SB_SLOTS, (u - 1) % _SB_SLOTS
            m3(t, prev)
            v2(t, (u - 3) % _SB_Z_SLOTS, prev, cur)
            m2(prev, cur)
            v1((u - 1) % _SB_Z_SLOTS, cur)
            m1(t, u)
        return carry

    lax.fori_loop(0, n_steps // _SB_Z_SLOTS, outer, 0)
    for qi in range(nq):
        acc = jnp.concatenate([acc_ref[qi, 0], acc_ref[qi, 1]], axis=0)
        o_ref[qi * tq:(qi + 1) * tq, :] = acc.T.astype(o_ref.dtype)


def _sb_attn(qt, k, vt, *, tq):
    b, hp, nq, _, _ = qt.shape
    s = k.shape[1]
    tab = _sb_item_table(nq)
    n_steps = tab.shape[1]
    u_mat = _suffix_matrix(tq)
    krow = lax.broadcasted_iota(jnp.int32, (tq, tq), 0)
    qcol = lax.broadcasted_iota(jnp.int32, (tq, tq), 1)
    bias = jnp.stack([jnp.zeros((tq, tq), F32), jnp.where(krow < qcol, 0.0, _SB_MASK_BIAS)])
    tr = pl.BlockSpec((None, None, nq, LANES, tq), lambda bi, hi, tab: (bi, hi, 0, 0, 0))
    tokb = pl.BlockSpec((None, s, LANES), lambda bi, hi, tab: (bi, 0, hi))
    grid_spec = pltpu.PrefetchScalarGridSpec(
        num_scalar_prefetch=1,
        grid=(b, hp),
        in_specs=[tr, tokb, tr,
                  pl.BlockSpec(u_mat.shape, lambda bi, hi, tab: (0, 0), pipeline_mode=pl.Buffered(1)),
                  pl.BlockSpec(bias.shape, lambda bi, hi, tab: (0, 0, 0), pipeline_mode=pl.Buffered(1))],
        out_specs=tokb,
        scratch_shapes=[
            pltpu.VMEM((_SB_Z_SLOTS, 2, tq, tq), F32),
            pltpu.VMEM((_SB_SLOTS, 2, tq, tq), BF16),
            pltpu.VMEM((_SB_SLOTS, 2, tq + SUBLANES, tq), F32),
            pltpu.VMEM((_SB_SLOTS, 2, tq, tq), BF16),
            pltpu.VMEM((nq + 1, 2, SB_HEAD_DIM, tq), F32),
            pltpu.VMEM((2, SUBLANES, tq), F32),
        ],
    )
    return pl.pallas_call(
        functools.partial(_sb_attn_kernel, tq=tq, n_steps=n_steps),
        grid_spec=grid_spec,
        out_shape=jax.ShapeDtypeStruct((b, s, hp * LANES), BF16),
        compiler_params=_params(2),
        name="sb_attn",
    )(jnp.asarray(tab), qt, k, vt, u_mat, bias)


def _ssm_kernel(u_ref, perm_ref, permt_ref, bs_ref, are_ref, aim_ref, cs_ref, d_ref, wg_ref, bg_ref,
                o_ref, bu_ref, xs_ref, *, t_steps, n_state):
    nb = u_ref.shape[0]
    n_slab = u_ref.shape[2] // LANES
    sw = n_state // n_slab

    @pl.when(pl.program_id(0) == 0)
    def _():
        xs_ref[...] = jnp.zeros_like(xs_ref)

    u_bt = u_ref[...].reshape(nb * t_steps, u_ref.shape[2])
    u = _dot(perm_ref[...], u_bt)
    ub = u.astype(BF16)
    for s in range(n_slab):
        bu = _dot(ub[:, s * LANES:(s + 1) * LANES], bs_ref[s])
        bu_ref[:, s * sw:(s + 1) * sw] = bu[:, :sw]
        bu_ref[:, n_state + s * sw:n_state + (s + 1) * sw] = bu[:, sw:]

    n_part = 2
    pw = n_state // n_part
    for part in range(n_part):
        re = slice(part * pw, (part + 1) * pw)
        im = slice(n_state + part * pw, n_state + (part + 1) * pw)
        a_re = are_ref[:, re]
        a_im = aim_ref[:, re]

        def body(t, carry, re=re, im=im, a_re=a_re, a_im=a_im):
            x_re, x_im = carry
            rows = pl.ds(pl.multiple_of(t * nb, nb), nb)
            n_re = a_re * x_re - a_im * x_im + bu_ref[rows, re]
            n_im = a_re * x_im + a_im * x_re + bu_ref[rows, im]
            bu_ref[rows, re] = n_re
            bu_ref[rows, im] = n_im
            return n_re, n_im

        x_re, x_im = lax.fori_loop(0, t_steps, body, (xs_ref[:, re], xs_ref[:, im]), unroll=2)
        xs_ref[:, re] = x_re
        xs_ref[:, im] = x_im

    ys = []
    for s in range(n_slab):
        x_re = bu_ref[:, s * sw:(s + 1) * sw].astype(BF16)
        x_im = bu_ref[:, n_state + s * sw:n_state + (s + 1) * sw].astype(BF16)
        ys.append(_dot(x_re, cs_ref[s, :sw, :]) + _dot(x_im, cs_ref[s, sw:, :]))
    y = jnp.concatenate(ys, axis=1) + d_ref[...] * u
    y = jax.nn.gelu(y, approximate=True)
    gate = jax.nn.sigmoid(_dot(y.astype(BF16), wg_ref[...]) + bg_ref[...])
    out = _dot(permt_ref[...], (y * gate).astype(BF16))
    o_ref[...] = out.astype(o_ref.dtype).reshape(o_ref.shape)


def _ssm(u, b_slab, lam_re, lam_im, c_slab, d_skip, w_glu, b_glu, *, t_steps):
    batch, s, ssw = u.shape
    n_state = lam_re.shape[1]
    assert batch == SUBLANES, "the scan puts the batch on the sublanes"
    blk = t_steps * batch
    r = np.arange(blk)
    perm = np.zeros((blk, blk), np.float32)
    perm[r, (r % batch) * t_steps + r // batch] = 1.0
    perm = jnp.asarray(perm, BF16)
    tblk = pl.BlockSpec((batch, t_steps, ssw), lambda i: (0, i, 0))
    return pl.pallas_call(
        functools.partial(_ssm_kernel, t_steps=t_steps, n_state=n_state),
        grid=(s // t_steps,),
        in_specs=[
            tblk, _const_spec(perm.shape), _const_spec(perm.shape),
            _const_spec(b_slab.shape), _const_spec(lam_re.shape), _const_spec(lam_im.shape),
            _const_spec(c_slab.shape), _const_spec(d_skip.shape), _const_spec(w_glu.shape),
            _const_spec(b_glu.shape),
        ],
        out_specs=tblk,
        out_shape=jax.ShapeDtypeStruct((batch, s, ssw), BF16),
        scratch_shapes=[pltpu.VMEM((blk, 2 * n_state), F32),
                        pltpu.VMEM((batch, 2 * n_state), F32)],
        compiler_params=_params(1),
        name="ssm",
    )(u, perm, perm.T, b_slab, lam_re, lam_im, c_slab, d_skip, w_glu, b_glu)


def _merge_kernel(oa_ref, os_ref, ga_ref, gs_ref, x_ref, wa_ref, ws_ref, wo_ref, g_ref, out_ref):
    pa = _dot(oa_ref[...], wa_ref[...])
    ps = _dot(os_ref[...], ws_ref[...])
    merged = ga_ref[...].astype(F32) * pa + gs_ref[...].astype(F32) * ps
    y = _dot(merged.astype(BF16), wo_ref[...])
    out_ref[...] = x_ref[...] + _rms(y, g_ref[...])


def _merge_out(o_attn, o_ssm, ga, gs, x, wa, ws, wo, gain, *, ts):
    b, s, d = x.shape
    sbw = o_attn.shape[2]
    ssw = o_ssm.shape[2]
    tok = lambda w: pl.BlockSpec((None, ts, w), lambda bi, ti: (bi, ti, 0))
    return pl.pallas_call(
        _merge_kernel,
        grid=(b, s // ts),
        in_specs=[tok(sbw), tok(ssw), tok(d), tok(d), tok(d),
                  _const_spec(wa.shape), _const_spec(ws.shape), _const_spec(wo.shape),
                  _const_spec(gain.shape)],
        out_specs=tok(d),
        out_shape=jax.ShapeDtypeStruct((b, s, d), F32),
        compiler_params=_params(2),
        name="merge_out",
    )(o_attn, o_ssm, ga, gs, x, wa, ws, wo, gain)


def _memkv_kernel(m_ref, g_ref, wk_ref, wv_ref, k_ref, v_ref):
    mn = _rms(m_ref[...], g_ref[...]).astype(BF16)
    k_ref[...] = _dot(mn, wk_ref[...]).astype(BF16)
    v_ref[...] = _dot(mn, wv_ref[...]).astype(BF16)


def _mem_kv(mem, gain, wk, wv):
    b, m, d = mem.shape
    blk = pl.BlockSpec((None, m, d), lambda bi: (bi, 0, 0))
    out = jax.ShapeDtypeStruct((b, m, d), BF16)
    return pl.pallas_call(
        _memkv_kernel,
        grid=(b,),
        in_specs=[blk, _const_spec(gain.shape), _const_spec(wk.shape), _const_spec(wv.shape)],
        out_specs=(blk, blk),
        out_shape=(out, out),
        compiler_params=_params(1),
        name="mem_kv",
    )(mem, gain, wk, wv)


def _xattn_kernel(x_ref, gpre_ref, wq_ref, k_ref, v_ref, wo_ref, gpost_ref, out_ref, *, heads):
    x = x_ref[...]
    d = x.shape[1]
    hd = d // heads
    h = _rms(x, gpre_ref[...]).astype(BF16)
    q = (_dot(h, wq_ref[...]) * (hd ** -0.5)).astype(BF16)
    outs = []
    for i in range(heads):
        sl = slice(i * hd, (i + 1) * hd)
        s = _dot_nt(q[:, sl], k_ref[:, sl])
        p = jnp.exp(s - jnp.max(s, axis=-1, keepdims=True))
        inv = 1.0 / jnp.sum(p, axis=-1, keepdims=True)
        outs.append((_dot(p.astype(BF16), v_ref[:, sl]) * inv).astype(BF16))
    y = _dot(jnp.concatenate(outs, axis=1), wo_ref[...])
    out_ref[...] = x + _rms(y, gpost_ref[...])


def _xattn(x, gpre, wq, kx, vx, wo, gpost, *, ts):
    b, s, d = x.shape
    m = kx.shape[1]
    tok = pl.BlockSpec((None, ts, d), lambda bi, ti: (bi, ti, 0))
    memb = pl.BlockSpec((None, m, d), lambda bi, ti: (bi, 0, 0))
    return pl.pallas_call(
        functools.partial(_xattn_kernel, heads=XA_HEADS),
        grid=(b, s // ts),
        in_specs=[tok, _const_spec(gpre.shape), _const_spec(wq.shape), memb, memb,
                  _const_spec(wo.shape), _const_spec(gpost.shape)],
        out_specs=tok,
        out_shape=jax.ShapeDtypeStruct((b, s, d), F32),
        compiler_params=_params(2),
        name="xattn",
    )(x, gpre, wq, kx, vx, wo, gpost)


def _ffn_kernel(x_ref, gpre_ref, wup_ref, cw_ref, cb_ref, wdn_ref, gpost_ref, out_ref,
                carry_ref, acc_ref, h_ref, up_ref, a_ref, *, d_ff, cwid):
    ts = x_ref.shape[0]
    halo = SUBLANES
    n_chunk = d_ff // cwid

    @pl.when(pl.program_id(1) == 0)
    def _():
        carry_ref[...] = jnp.zeros_like(carry_ref)

    h_ref[...] = _rms(x_ref[...], gpre_ref[...]).astype(BF16)

    def stage_up(c, slot):
        for half in range(2):
            c0 = half * d_ff + c * cwid
            up_ref[slot, half] = _dot(h_ref[...], wup_ref[:, c0:c0 + cwid])

    def conv(c0, up):
        ext = jnp.concatenate([carry_ref[:, c0:c0 + cwid], up], axis=0)
        carry_ref[:, c0:c0 + cwid] = up[ts - halo:, :]
        out = cb_ref[:, c0:c0 + cwid]
        for j in range(CONV_WIDTH):
            off = halo - (CONV_WIDTH - 1) + j
            out = out + cw_ref[j:j + 1, c0:c0 + cwid] * ext[off:off + ts, :]
        return out

    def stage_act(c, slot):
        gate = conv(c * cwid, up_ref[slot, 0])
        val = conv(d_ff + c * cwid, up_ref[slot, 1])
        a_ref[slot] = (jax.nn.gelu(gate, approximate=True) * val).astype(BF16)

    def stage_down(c, slot):
        part = _dot(a_ref[slot], wdn_ref[c * cwid:(c + 1) * cwid, :])
        acc_ref[...] = part if c == 0 else acc_ref[...] + part

    for i in range(n_chunk + 2):
        if i >= 2:
            stage_down(i - 2, i % 2)
        if 1 <= i <= n_chunk:
            stage_act(i - 1, (i - 1) % 2)
        if i < n_chunk:
            stage_up(i, i % 2)
    out_ref[...] = x_ref[...] + _rms(acc_ref[...], gpost_ref[...])


def _conv_ffn(x, gpre, w_up, conv_w, conv_b, w_dn, gpost, *, ts, cwid):
    b, s, d = x.shape
    d_ff = w_dn.shape[0]
    tok = pl.BlockSpec((None, ts, d), lambda bi, ti: (bi, ti, 0))
    return pl.pallas_call(
        functools.partial(_ffn_kernel, d_ff=d_ff, cwid=cwid),
        grid=(b, s // ts),
        in_specs=[tok, _const_spec(gpre.shape), _const_spec(w_up.shape), _const_spec(conv_w.shape),
                  _const_spec(conv_b.shape), _const_spec(w_dn.shape), _const_spec(gpost.shape)],
        out_specs=tok,
        out_shape=jax.ShapeDtypeStruct((b, s, d), F32),
        scratch_shapes=[pltpu.VMEM((SUBLANES, 2 * d_ff), F32),
                        pltpu.VMEM((ts, d), F32),
                        pltpu.VMEM((ts, d), BF16),
                        pltpu.VMEM((2, 2, ts, cwid), F32),
                        pltpu.VMEM((2, ts, cwid), BF16)],
        compiler_params=_params(2),
        name="conv_ffn",
    )(x, gpre, w_up, conv_w, conv_b, w_dn, gpost)


def _ssm_slabs(bb_re, bb_im, c_re, c_im):
    g, p, c = bb_re.shape
    gps = LANES // c
    n_slab = g // gps
    eye = jnp.eye(gps, dtype=F32)

    def b_part(bb):
        t = bb.reshape(n_slab, gps, p, c)
        return jnp.einsum('sgpc,gh->sgchp', t, eye).reshape(n_slab, gps * c, gps * p)

    def c_part(cc):
        t = cc.reshape(n_slab, gps, c, p)
        return jnp.einsum('sgcp,gh->shpgc', t, eye).reshape(n_slab, gps * p, gps * c)

    b_slab = jnp.concatenate([b_part(bb_re), b_part(bb_im)], axis=2).astype(BF16)
    c_slab = jnp.concatenate([c_part(c_re), c_part(-c_im)], axis=1).astype(BF16)
    return b_slab, c_slab


def _layer(x, mem, l, p):
    b, s, d = x.shape
    sbw = p["w_branch_attn"].shape[1]
    ssw = p["w_branch_ssm"].shape[1]
    row = lambda a: a[l][None, :].astype(F32)
    wb = lambda a: a[l].astype(BF16)

    lam_re, lam_im, bb_re, bb_im = _discretize(
        p["ssm_a_re"][l], p["ssm_a_im"][l], p["ssm_log_dt"][l], p["ssm_b_re"][l], p["ssm_b_im"][l])
    b_slab, c_slab = _ssm_slabs(bb_re, bb_im, p["ssm_c_re"][l].astype(F32), p["ssm_c_im"][l].astype(F32))
    n_state = lam_re.size
    lam_re_b = jnp.broadcast_to(lam_re.reshape(1, n_state), (SUBLANES, n_state))
    lam_im_b = jnp.broadcast_to(lam_im.reshape(1, n_state), (SUBLANES, n_state))

    qt, k, vt, u, ga, gs = _in_proj(x, row(p["norm_mix_pre"]), wb(p["w_in"]), row(p["b_gate"]),
                                       sbw=sbw, ssw=ssw, ts=512, tq=256)
    o_attn = _sb_attn(qt, k, vt, tq=256)
    o_ssm = _ssm(u, b_slab, lam_re_b, lam_im_b, c_slab,
                 row(p["ssm_d"]), wb(p["ssm_w_glu"]), row(p["ssm_b_glu"]), t_steps=64)
    x = _merge_out(o_attn, o_ssm, ga, gs, x, wb(p["w_branch_attn"]),
                   wb(p["w_branch_ssm"]), wb(p["w_out"]), row(p["norm_mix_post"]), ts=512)

    kx, vx = _mem_kv(mem, row(p["norm_mem"]), wb(p["xa_wk"]), wb(p["xa_wv"]))
    x = _xattn(x, row(p["norm_xa_pre"]), wb(p["xa_wq"]), kx, vx, wb(p["xa_wo"]),
               row(p["norm_xa_post"]), ts=512)

    x = _conv_ffn(x, row(p["norm_ffn_pre"]), wb(p["ffn_w_up"]), p["ffn_conv_w"][l].astype(F32),
                  row(p["ffn_conv_b"]), wb(p["ffn_w_down"]), row(p["norm_ffn_post"]), ts=512, cwid=256)
    return x


def kernel(x, mem, norm_mix_pre, norm_mix_post, w_in, b_gate, ssm_a_re, ssm_a_im, ssm_log_dt, ssm_b_re, ssm_b_im, ssm_c_re, ssm_c_im, ssm_d, ssm_w_glu, ssm_b_glu, w_branch_attn, w_branch_ssm, w_out, norm_xa_pre, norm_xa_post, norm_mem, xa_wq, xa_wk, xa_wv, xa_wo, norm_ffn_pre, norm_ffn_post, ffn_w_up, ffn_conv_w, ffn_conv_b, ffn_w_down):
    p = dict(norm_mix_pre=norm_mix_pre, norm_mix_post=norm_mix_post, w_in=w_in, b_gate=b_gate,
             ssm_a_re=ssm_a_re, ssm_a_im=ssm_a_im, ssm_log_dt=ssm_log_dt, ssm_b_re=ssm_b_re,
             ssm_b_im=ssm_b_im, ssm_c_re=ssm_c_re, ssm_c_im=ssm_c_im, ssm_d=ssm_d,
             ssm_w_glu=ssm_w_glu, ssm_b_glu=ssm_b_glu, w_branch_attn=w_branch_attn,
             w_branch_ssm=w_branch_ssm, w_out=w_out, norm_xa_pre=norm_xa_pre,
             norm_xa_post=norm_xa_post, norm_mem=norm_mem, xa_wq=xa_wq, xa_wk=xa_wk, xa_wv=xa_wv,
             xa_wo=xa_wo, norm_ffn_pre=norm_ffn_pre, norm_ffn_post=norm_ffn_post,
             ffn_w_up=ffn_w_up, ffn_conv_w=ffn_conv_w, ffn_conv_b=ffn_conv_b, ffn_w_down=ffn_w_down)
    for l in range(w_in.shape[0]):
        x = _layer(x, mem, l, p)
    return x
```

```python
import functools
import math

import numpy as np

import jax
import jax.numpy as jnp
from jax import lax
from jax.experimental import pallas as pl
from jax.experimental.pallas import tpu as pltpu

F32 = jnp.float32
BF16 = jnp.bfloat16

RMS_EPS = 1e-6
SB_HEAD_DIM = 64
SSM_GROUP = 16
SSM_STATE = 64
XA_HEADS = 4
CONV_WIDTH = 3
LANES = 128
SUBLANES = 8
VMEM_LIMIT = 56 * 1024 * 1024
LOG2E = 1.4426950408889634


def _rms(x, g):
    ms = jnp.mean(x * x, axis=-1, keepdims=True)
    return x * lax.rsqrt(ms + RMS_EPS) * g


def _dot(a, b):
    return jnp.dot(a, b, preferred_element_type=F32)


def _gelu_tanh(x):
    a = -2.0 * math.sqrt(2.0 / math.pi) * LOG2E
    return x / (1.0 + jnp.exp2(x * (a + (a * 0.044715) * (x * x))))


def _dot_nt(a, b):
    return lax.dot_general(a, b, (((1,), (1,)), ((), ())), preferred_element_type=F32)


def _params(n_axes):
    return pltpu.CompilerParams(
        dimension_semantics=("arbitrary",) * n_axes, vmem_limit_bytes=VMEM_LIMIT)


def _const_spec(shape):
    zeros = (0,) * len(shape)
    return pl.BlockSpec(shape, lambda *_: zeros, pipeline_mode=pl.Buffered(1))


def _discretize_kernel(are_ref, aim_ref, ldt_ref, bre_ref, bim_ref,
                       lre_ref, lim_ref, bbre_ref, bbim_ref):
    a_re = are_ref[...]
    a_im = aim_ref[...]
    dt = jnp.exp(ldt_ref[...])
    mag = jnp.exp(a_re * dt)
    l_re = mag * jnp.cos(a_im * dt)
    l_im = mag * jnp.sin(a_im * dt)
    lre_ref[...] = l_re
    lim_ref[...] = l_im
    n_re = l_re - 1.0
    inv = 1.0 / (a_re * a_re + a_im * a_im)
    c_re = (n_re * a_re + l_im * a_im) * inv
    c_im = (l_im * a_re - n_re * a_im) * inv
    b_re = bre_ref[...]
    b_im = bim_ref[...]
    bbre_ref[...] = c_re * b_re - c_im * b_im
    bbim_ref[...] = c_re * b_im + c_im * b_re


def _discretize(a_re, a_im, log_dt, b_re, b_im):
    g, p, c = b_re.shape
    rep = lambda a: jnp.repeat(a, c, axis=-1)
    args = (rep(a_re), rep(a_im), jnp.broadcast_to(log_dt[:, None], (g, p * c)),
            b_re.reshape(g, p * c), b_im.reshape(g, p * c))
    out = jax.ShapeDtypeStruct((g, p * c), F32)
    l_re, l_im, bb_re, bb_im = pl.pallas_call(
        _discretize_kernel, out_shape=(out, out, out, out), name="ssm_discretize")(*args)
    return (l_re[:, ::c], l_im[:, ::c], bb_re.reshape(g, p, c), bb_im.reshape(g, p, c))


def _inproj_kernel(x_ref, g_ref, w_ref, bg_ref, qt_ref, k_ref, vt_ref, u_ref, ga_ref, gs_ref,
                   *, sbw, ssw, d, qscale, tq):
    ts = x_ref.shape[0]
    h = _rms(x_ref[...], g_ref[...]).astype(BF16)

    def proj(c0, n):
        return _dot(h, w_ref[:, c0:c0 + n])

    q = proj(0, sbw) * qscale
    v = proj(2 * sbw, sbw)
    for hp in range(sbw // LANES):
        cols = slice(hp * LANES, (hp + 1) * LANES)
        for c in range(ts // tq):
            rows = slice(c * tq, (c + 1) * tq)
            qt_ref[hp, c] = q[rows, cols].T.astype(BF16)
            vt_ref[hp, c] = v[rows, cols].T.astype(BF16)
    k_ref[...] = proj(sbw, sbw).astype(BF16)
    u_ref[...] = proj(3 * sbw, ssw).astype(BF16)
    g0 = 3 * sbw + ssw
    cw = 512
    for c in range(d // cw):
        sl = slice(c * cw, (c + 1) * cw)
        ga_ref[:, sl] = jax.nn.sigmoid(proj(g0 + c * cw, cw) + bg_ref[:, sl]).astype(BF16)
        gs_ref[:, sl] = jax.nn.sigmoid(
            proj(g0 + d + c * cw, cw) + bg_ref[:, d + c * cw:d + (c + 1) * cw]).astype(BF16)


def _in_proj(x, gain, w_in, b_gate, *, sbw, ssw, ts, tq):
    b, s, d = x.shape
    in_w = w_in.shape[1]
    hp = sbw // LANES
    qscale = SB_HEAD_DIM ** -0.5 * LOG2E
    tok = lambda w: pl.BlockSpec((None, ts, w), lambda bi, ti: (bi, ti, 0))
    tr = pl.BlockSpec((None, hp, ts // tq, LANES, tq), lambda bi, ti: (bi, 0, ti, 0, 0))
    out_shape = (
        jax.ShapeDtypeStruct((b, hp, s // tq, LANES, tq), BF16),
        jax.ShapeDtypeStruct((b, s, sbw), BF16),
        jax.ShapeDtypeStruct((b, hp, s // tq, LANES, tq), BF16),
        jax.ShapeDtypeStruct((b, s, ssw), BF16),
        jax.ShapeDtypeStruct((b, s, d), BF16),
        jax.ShapeDtypeStruct((b, s, d), BF16),
    )
    return pl.pallas_call(
        functools.partial(_inproj_kernel, sbw=sbw, ssw=ssw, d=d, qscale=qscale, tq=tq),
        grid=(b, s // ts),
        in_specs=[tok(d), _const_spec((1, d)), _const_spec((d, in_w)), _const_spec((1, 2 * d))],
        out_specs=(tr, tok(sbw), tr, tok(ssw), tok(d), tok(d)),
        out_shape=out_shape,
        compiler_params=_params(2),
        name="in_proj",
    )(x, gain, w_in, b_gate)


_SB_DEPTH = 5
_SB_Z_SLOTS = 4
_SB_SLOTS = 2
_SB_MASK_BIAS = -1e30
_SB_EXP_CAP = 30.0


def _suffix_matrix(nk):
    r = lax.broadcasted_iota(jnp.int32, (nk + SUBLANES, nk), 0)
    c = lax.broadcasted_iota(jnp.int32, (nk + SUBLANES, nk), 1)
    return ((c >= r) | (r >= nk)).astype(BF16)


def _sb_item_table(nq):
    items = [(qi, qi - j, int(j == 0), int(j == qi)) for qi in range(nq) for j in range(qi + 1)]
    n = len(items)
    n_steps = -(-(n + _SB_DEPTH - 1) // _SB_Z_SLOTS) * _SB_Z_SLOTS
    tab = np.zeros((8, n_steps), np.int32)
    for t in range(n_steps):
        qi, c, first, _ = items[min(t, n - 1)]
        tab[0:3, t] = (qi, c, first if t < n else 0)
        if 0 <= t - 3 < n:
            tab[3, t] = items[t - 3][2]
        tab[4:8, t] = items[t - 4] if 0 <= t - 4 < n else (nq, 0, 0, 0)
    return tab


def _sb_attn_kernel(tab_ref, qt_ref, k_ref, vt_ref, u_ref, bias_ref, o_ref,
                    z_ref, sp_ref, ct_ref, w_ref, acc_ref, r_ref, *, tq, n_steps):
    nq = qt_ref.shape[0]
    drow = lax.broadcasted_iota(jnp.int32, (LANES, tq), 0)
    head_rows = [(drow // SB_HEAD_DIM) == h for h in range(2)]

    for ref in (z_ref, sp_ref, ct_ref, w_ref, acc_ref, r_ref):
        ref[...] = jnp.zeros_like(ref)

    def m1(t, zs):
        qt = qt_ref[tab_ref[0, t]]
        ks = pl.multiple_of(tab_ref[1, t] * tq, tq)
        kc = k_ref[pl.ds(ks, tq), :]
        bias = bias_ref[tab_ref[2, t]]
        for h in range(2):
            z_ref[zs, h] = _dot(kc, jnp.where(head_rows[h], qt, jnp.zeros_like(qt))) + bias

    def v1(zs, s):
        for h in range(2):
            z = z_ref[zs, h]
            lg = jnp.log(1.0 + jnp.exp2(jnp.minimum(z, _SB_EXP_CAP))) * LOG2E
            sp_ref[s, h] = jnp.maximum(z, lg).astype(BF16)

    def m2(s_in, s_out):
        for h in range(2):
            ct_ref[s_out, h] = _dot(u_ref[...], sp_ref[s_in, h])

    def v2(t, zs, s_in, s_out):
        first = tab_ref[3, t] > 0
        for h in range(2):
            r = jnp.where(first, 0.0, r_ref[h])
            w = jnp.exp2(z_ref[zs, h] - ct_ref[s_in, h, :tq, :] - r[0:1, :])
            w_ref[s_out, h] = w.astype(BF16)
            r_ref[h] = r + ct_ref[s_in, h, tq:, :]

    def m3(t, s_in):
        qi = tab_ref[4, t]
        first = tab_ref[6, t] > 0
        vc = vt_ref[tab_ref[5, t]]
        for h in range(2):
            pv = _dot(vc[h * SB_HEAD_DIM:(h + 1) * SB_HEAD_DIM, :], w_ref[s_in, h])
            acc_ref[qi, h] = jnp.where(first, 0.0, acc_ref[qi, h]) + pv

    def outer(i, carry):
        for u in range(_SB_Z_SLOTS):
            t = i * _SB_Z_SLOTS + u
            cur, prev = u % _SB_SLOTS, (u - 1) % _SB_SLOTS
            m3(t, prev)
            v2(t, (u - 3) % _SB_Z_SLOTS, prev, cur)
            m2(prev, cur)
            v1((u - 1) % _SB_Z_SLOTS, cur)
            m1(t, u)
        return carry

    lax.fori_loop(0, n_steps // _SB_Z_SLOTS, outer, 0)
    for qi in range(nq):
        acc = jnp.concatenate([acc_ref[qi, 0], acc_ref[qi, 1]], axis=0)
        o_ref[qi * tq:(qi + 1) * tq, :] = acc.T.astype(o_ref.dtype)


def _sb_attn(qt, k, vt, *, tq):
    b, hp, nq, _, _ = qt.shape
    s = k.shape[1]
    tab = _sb_item_table(nq)
    n_steps = tab.shape[1]
    u_mat = _suffix_matrix(tq)
    krow = lax.broadcasted_iota(jnp.int32, (tq, tq), 0)
    qcol = lax.broadcasted_iota(jnp.int32, (tq, tq), 1)
    bias = jnp.stack([jnp.zeros((tq, tq), F32), jnp.where(krow < qcol, 0.0, _SB_MASK_BIAS)])
    tr = pl.BlockSpec((None, None, nq, LANES, tq), lambda bi, hi, tab: (bi, hi, 0, 0, 0))
    tokb = pl.BlockSpec((None, s, LANES), lambda bi, hi, tab: (bi, 0, hi))
    grid_spec = pltpu.PrefetchScalarGridSpec(
        num_scalar_prefetch=1,
        grid=(b, hp),
        in_specs=[tr, tokb, tr,
                  pl.BlockSpec(u_mat.shape, lambda bi, hi, tab: (0, 0), pipeline_mode=pl.Buffered(1)),
                  pl.BlockSpec(bias.shape, lambda bi, hi, tab: (0, 0, 0), pipeline_mode=pl.Buffered(1))],
        out_specs=tokb,
        scratch_shapes=[
            pltpu.VMEM((_SB_Z_SLOTS, 2, tq, tq), F32),
            pltpu.VMEM((_SB_SLOTS, 2, tq, tq), BF16),
            pltpu.VMEM((_SB_SLOTS, 2, tq + SUBLANES, tq), F32),
            pltpu.VMEM((_SB_SLOTS, 2, tq, tq), BF16),
            pltpu.VMEM((nq + 1, 2, SB_HEAD_DIM, tq), F32),
            pltpu.VMEM((2, SUBLANES, tq), F32),
        ],
    )
    return pl.pallas_call(
        functools.partial(_sb_attn_kernel, tq=tq, n_steps=n_steps),
        grid_spec=grid_spec,
        out_shape=jax.ShapeDtypeStruct((b, s, hp * LANES), BF16),
        compiler_params=_params(2),
        name="sb_attn",
    )(jnp.asarray(tab), qt, k, vt, u_mat, bias)


def _ssm_kernel(u_ref, perm_ref, permt_ref, bs_ref, are_ref, aim_ref, cs_ref, d_ref, wg_ref, bg_ref,
                o_ref, bu_ref, xs_ref, *, t_steps, n_state):
    nb = u_ref.shape[0]
    n_slab = u_ref.shape[2] // LANES
    sw = n_state // n_slab

    @pl.when(pl.program_id(0) == 0)
    def _():
        xs_ref[...] = jnp.zeros_like(xs_ref)

    u_bt = u_ref[...].reshape(nb * t_steps, u_ref.shape[2])
    u = _dot(perm_ref[...], u_bt)
    ub = u.astype(BF16)
    for s in range(n_slab):
        bu = _dot(ub[:, s * LANES:(s + 1) * LANES], bs_ref[s])
        bu_ref[:, s * sw:(s + 1) * sw] = bu[:, :sw]
        bu_ref[:, n_state + s * sw:n_state + (s + 1) * sw] = bu[:, sw:]

    n_part = 2
    pw = n_state // n_part
    for part in range(n_part):
        re = slice(part * pw, (part + 1) * pw)
        im = slice(n_state + part * pw, n_state + (part + 1) * pw)
        a_re = are_ref[:, re]
        a_im = aim_ref[:, re]

        def body(t, carry, re=re, im=im, a_re=a_re, a_im=a_im):
            x_re, x_im = carry
            rows = pl.ds(pl.multiple_of(t * nb, nb), nb)
            n_re = a_re * x_re - a_im * x_im + bu_ref[rows, re]
            n_im = a_re * x_im + a_im * x_re + bu_ref[rows, im]
            bu_ref[rows, re] = n_re
            bu_ref[rows, im] = n_im
            return n_re, n_im

        x_re, x_im = lax.fori_loop(0, t_steps, body, (xs_ref[:, re], xs_ref[:, im]), unroll=2)
        xs_ref[:, re] = x_re
        xs_ref[:, im] = x_im

    ys = []
    for s in range(n_slab):
        x_re = bu_ref[:, s * sw:(s + 1) * sw].astype(BF16)
        x_im = bu_ref[:, n_state + s * sw:n_state + (s + 1) * sw].astype(BF16)
        ys.append(_dot(x_re, cs_ref[s, :sw, :]) + _dot(x_im, cs_ref[s, sw:, :]))
    y = jnp.concatenate(ys, axis=1) + d_ref[...] * u
    y = _gelu_tanh(y)
    gate = jax.nn.sigmoid(_dot(y.astype(BF16), wg_ref[...]) + bg_ref[...])
    out = _dot(permt_ref[...], (y * gate).astype(BF16))
    o_ref[...] = out.astype(o_ref.dtype).reshape(o_ref.shape)


def _ssm(u, b_slab, lam_re, lam_im, c_slab, d_skip, w_glu, b_glu, *, t_steps):
    batch, s, ssw = u.shape
    n_state = lam_re.shape[1]
    assert batch == SUBLANES, "the scan puts the batch on the sublanes"
    blk = t_steps * batch
    r = np.arange(blk)
    perm = np.zeros((blk, blk), np.float32)
    perm[r, (r % batch) * t_steps + r // batch] = 1.0
    perm = jnp.asarray(perm, BF16)
    tblk = pl.BlockSpec((batch, t_steps, ssw), lambda i: (0, i, 0))
    return pl.pallas_call(
        functools.partial(_ssm_kernel, t_steps=t_steps, n_state=n_state),
        grid=(s // t_steps,),
        in_specs=[
            tblk, _const_spec(perm.shape), _const_spec(perm.shape),
            _const_spec(b_slab.shape), _const_spec(lam_re.shape), _const_spec(lam_im.shape),
            _const_spec(c_slab.shape), _const_spec(d_skip.shape), _const_spec(w_glu.shape),
            _const_spec(b_glu.shape),
        ],
        out_specs=tblk,
        out_shape=jax.ShapeDtypeStruct((batch, s, ssw), BF16),
        scratch_shapes=[pltpu.VMEM((blk, 2 * n_state), F32),
                        pltpu.VMEM((batch, 2 * n_state), F32)],
        compiler_params=_params(1),
        name="ssm",
    )(u, perm, perm.T, b_slab, lam_re, lam_im, c_slab, d_skip, w_glu, b_glu)


def _merge_kernel(oa_ref, os_ref, ga_ref, gs_ref, x_ref, wa_ref, ws_ref, wo_ref, g_ref, out_ref):
    pa = _dot(oa_ref[...], wa_ref[...])
    ps = _dot(os_ref[...], ws_ref[...])
    merged = ga_ref[...].astype(F32) * pa + gs_ref[...].astype(F32) * ps
    y = _dot(merged.astype(BF16), wo_ref[...])
    out_ref[...] = x_ref[...] + _rms(y, g_ref[...])


def _merge_out(o_attn, o_ssm, ga, gs, x, wa, ws, wo, gain, *, ts):
    b, s, d = x.shape
    sbw = o_attn.shape[2]
    ssw = o_ssm.shape[2]
    tok = lambda w: pl.BlockSpec((None, ts, w), lambda bi, ti: (bi, ti, 0))
    return pl.pallas_call(
        _merge_kernel,
        grid=(b, s // ts),
        in_specs=[tok(sbw), tok(ssw), tok(d), tok(d), tok(d),
                  _const_spec(wa.shape), _const_spec(ws.shape), _const_spec(wo.shape),
                  _const_spec(gain.shape)],
        out_specs=tok(d),
        out_shape=jax.ShapeDtypeStruct((b, s, d), F32),
        compiler_params=_params(2),
        name="merge_out",
    )(o_attn, o_ssm, ga, gs, x, wa, ws, wo, gain)


def _memkv_kernel(m_ref, g_ref, wk_ref, wv_ref, k_ref, v_ref):
    mn = _rms(m_ref[...], g_ref[...]).astype(BF16)
    k_ref[...] = _dot(mn, wk_ref[...]).astype(BF16)
    v_ref[...] = _dot(mn, wv_ref[...]).astype(BF16)


def _mem_kv(mem, gain, wk, wv):
    b, m, d = mem.shape
    blk = pl.BlockSpec((None, m, d), lambda bi: (bi, 0, 0))
    out = jax.ShapeDtypeStruct((b, m, d), BF16)
    return pl.pallas_call(
        _memkv_kernel,
        grid=(b,),
        in_specs=[blk, _const_spec(gain.shape), _const_spec(wk.shape), _const_spec(wv.shape)],
        out_specs=(blk, blk),
        out_shape=(out, out),
        compiler_params=_params(1),
        name="mem_kv",
    )(mem, gain, wk, wv)


def _xattn_kernel(x_ref, gpre_ref, wq_ref, k_ref, v_ref, wo_ref, gpost_ref, out_ref, *, heads):
    x = x_ref[...]
    d = x.shape[1]
    hd = d // heads
    h = _rms(x, gpre_ref[...]).astype(BF16)
    q = (_dot(h, wq_ref[...]) * (hd ** -0.5)).astype(BF16)
    outs = []
    for i in range(heads):
        sl = slice(i * hd, (i + 1) * hd)
        s = _dot_nt(q[:, sl], k_ref[:, sl])
        p = jnp.exp(s - jnp.max(s, axis=-1, keepdims=True))
        inv = 1.0 / jnp.sum(p, axis=-1, keepdims=True)
        outs.append((_dot(p.astype(BF16), v_ref[:, sl]) * inv).astype(BF16))
    y = _dot(jnp.concatenate(outs, axis=1), wo_ref[...])
    out_ref[...] = x + _rms(y, gpost_ref[...])


def _xattn(x, gpre, wq, kx, vx, wo, gpost, *, ts):
    b, s, d = x.shape
    m = kx.shape[1]
    tok = pl.BlockSpec((None, ts, d), lambda bi, ti: (bi, ti, 0))
    memb = pl.BlockSpec((None, m, d), lambda bi, ti: (bi, 0, 0))
    return pl.pallas_call(
        functools.partial(_xattn_kernel, heads=XA_HEADS),
        grid=(b, s // ts),
        in_specs=[tok, _const_spec(gpre.shape), _const_spec(wq.shape), memb, memb,
                  _const_spec(wo.shape), _const_spec(gpost.shape)],
        out_specs=tok,
        out_shape=jax.ShapeDtypeStruct((b, s, d), F32),
        compiler_params=_params(2),
        name="xattn",
    )(x, gpre, wq, kx, vx, wo, gpost)


def _ffn_kernel(x_ref, gpre_ref, wup_ref, cw_ref, cb_ref, wdn_ref, gpost_ref, out_ref,
                carry_ref, acc_ref, h_ref, up_ref, a_ref, *, d_ff, cwid):
    ts = x_ref.shape[0]
    halo = SUBLANES
    n_chunk = d_ff // cwid

    @pl.when(pl.program_id(1) == 0)
    def _():
        carry_ref[...] = jnp.zeros_like(carry_ref)

    h_ref[...] = _rms(x_ref[...], gpre_ref[...]).astype(BF16)

    def stage_up(c, slot):
        for half in range(2):
            c0 = half * d_ff + c * cwid
            up_ref[slot, half] = _dot(h_ref[...], wup_ref[:, c0:c0 + cwid])

    def conv(c0, up):
        ext = jnp.concatenate([carry_ref[:, c0:c0 + cwid], up], axis=0)
        carry_ref[:, c0:c0 + cwid] = up[ts - halo:, :]
        out = cb_ref[:, c0:c0 + cwid] + cw_ref[CONV_WIDTH - 1:CONV_WIDTH, c0:c0 + cwid] * up
        for back in range(1, CONV_WIDTH):
            tap = pltpu.roll(ext, back, 0)[halo:, :]
            out = out + cw_ref[CONV_WIDTH - 1 - back:CONV_WIDTH - back, c0:c0 + cwid] * tap
        return out

    def stage_act(c, slot):
        gate = conv(c * cwid, up_ref[slot, 0])
        val = conv(d_ff + c * cwid, up_ref[slot, 1])
        a_ref[slot] = (_gelu_tanh(gate) * val).astype(BF16)

    def stage_down(c, slot):
        part = _dot(a_ref[slot], wdn_ref[c * cwid:(c + 1) * cwid, :])
        acc_ref[...] = part if c == 0 else acc_ref[...] + part

    for i in range(n_chunk + 2):
        if i >= 2:
            stage_down(i - 2, i % 2)
        if 1 <= i <= n_chunk:
            stage_act(i - 1, (i - 1) % 2)
        if i < n_chunk:
            stage_up(i, i % 2)
    out_ref[...] = x_ref[...] + _rms(acc_ref[...], gpost_ref[...])


def _conv_ffn(x, gpre, w_up, conv_w, conv_b, w_dn, gpost, *, ts, cwid):
    b, s, d = x.shape
    d_ff = w_dn.shape[0]
    tok = pl.BlockSpec((None, ts, d), lambda bi, ti: (bi, ti, 0))
    return pl.pallas_call(
        functools.partial(_ffn_kernel, d_ff=d_ff, cwid=cwid),
        grid=(b, s // ts),
        in_specs=[tok, _const_spec(gpre.shape), _const_spec(w_up.shape), _const_spec(conv_w.shape),
                  _const_spec(conv_b.shape), _const_spec(w_dn.shape), _const_spec(gpost.shape)],
        out_specs=tok,
        out_shape=jax.ShapeDtypeStruct((b, s, d), F32),
        scratch_shapes=[pltpu.VMEM((SUBLANES, 2 * d_ff), F32),
                        pltpu.VMEM((ts, d), F32),
                        pltpu.VMEM((ts, d), BF16),
                        pltpu.VMEM((2, 2, ts, cwid), F32),
                        pltpu.VMEM((2, ts, cwid), BF16)],
        compiler_params=_params(2),
        name="conv_ffn",
    )(x, gpre, w_up, conv_w, conv_b, w_dn, gpost)


def _ssm_slabs(bb_re, bb_im, c_re, c_im):
    g, p, c = bb_re.shape
    gps = LANES // c
    n_slab = g // gps
    eye = jnp.eye(gps, dtype=F32)

    def b_part(bb):
        t = bb.reshape(n_slab, gps, p, c)
        return jnp.einsum('sgpc,gh->sgchp', t, eye).reshape(n_slab, gps * c, gps * p)

    def c_part(cc):
        t = cc.reshape(n_slab, gps, c, p)
        return jnp.einsum('sgcp,gh->shpgc', t, eye).reshape(n_slab, gps * p, gps * c)

    b_slab = jnp.concatenate([b_part(bb_re), b_part(bb_im)], axis=2).astype(BF16)
    c_slab = jnp.concatenate([c_part(c_re), c_part(-c_im)], axis=1).astype(BF16)
    return b_slab, c_slab


def _layer(x, mem, l, p):
    b, s, d = x.shape
    sbw = p["w_branch_attn"].shape[1]
    ssw = p["w_branch_ssm"].shape[1]
    row = lambda a: a[l][None, :].astype(F32)
    wb = lambda a: a[l].astype(BF16)

    lam_re, lam_im, bb_re, bb_im = _discretize(
        p["ssm_a_re"][l], p["ssm_a_im"][l], p["ssm_log_dt"][l], p["ssm_b_re"][l], p["ssm_b_im"][l])
    b_slab, c_slab = _ssm_slabs(bb_re, bb_im, p["ssm_c_re"][l].astype(F32), p["ssm_c_im"][l].astype(F32))
    n_state = lam_re.size
    lam_re_b = jnp.broadcast_to(lam_re.reshape(1, n_state), (SUBLANES, n_state))
    lam_im_b = jnp.broadcast_to(lam_im.reshape(1, n_state), (SUBLANES, n_state))

    qt, k, vt, u, ga, gs = _in_proj(x, row(p["norm_mix_pre"]), wb(p["w_in"]), row(p["b_gate"]),
                                       sbw=sbw, ssw=ssw, ts=512, tq=256)
    o_attn = _sb_attn(qt, k, vt, tq=256)
    o_ssm = _ssm(u, b_slab, lam_re_b, lam_im_b, c_slab,
                 row(p["ssm_d"]), wb(p["ssm_w_glu"]), row(p["ssm_b_glu"]), t_steps=64)
    x = _merge_out(o_attn, o_ssm, ga, gs, x, wb(p["w_branch_attn"]),
                   wb(p["w_branch_ssm"]), wb(p["w_out"]), row(p["norm_mix_post"]), ts=512)

    kx, vx = _mem_kv(mem, row(p["norm_mem"]), wb(p["xa_wk"]), wb(p["xa_wv"]))
    x = _xattn(x, row(p["norm_xa_pre"]), wb(p["xa_wq"]), kx, vx, wb(p["xa_wo"]),
               row(p["norm_xa_post"]), ts=512)

    x = _conv_ffn(x, row(p["norm_ffn_pre"]), wb(p["ffn_w_up"]), p["ffn_conv_w"][l].astype(F32),
                  row(p["ffn_conv_b"]), wb(p["ffn_w_down"]), row(p["norm_ffn_post"]), ts=512, cwid=256)
    return x


def kernel(x, mem, norm_mix_pre, norm_mix_post, w_in, b_gate, ssm_a_re, ssm_a_im, ssm_log_dt, ssm_b_re, ssm_b_im, ssm_c_re, ssm_c_im, ssm_d, ssm_w_glu, ssm_b_glu, w_branch_attn, w_branch_ssm, w_out, norm_xa_pre, norm_xa_post, norm_mem, xa_wq, xa_wk, xa_wv, xa_wo, norm_ffn_pre, norm_ffn_post, ffn_w_up, ffn_conv_w, ffn_conv_b, ffn_w_down):
    p = dict(norm_mix_pre=norm_mix_pre, norm_mix_post=norm_mix_post, w_in=w_in, b_gate=b_gate,
             ssm_a_re=ssm_a_re, ssm_a_im=ssm_a_im, ssm_log_dt=ssm_log_dt, ssm_b_re=ssm_b_re,
             ssm_b_im=ssm_b_im, ssm_c_re=ssm_c_re, ssm_c_im=ssm_c_im, ssm_d=ssm_d,
             ssm_w_glu=ssm_w_glu, ssm_b_glu=ssm_b_glu, w_branch_attn=w_branch_attn,
             w_branch_ssm=w_branch_ssm, w_out=w_out, norm_xa_pre=norm_xa_pre,
             norm_xa_post=norm_xa_post, norm_mem=norm_mem, xa_wq=xa_wq, xa_wk=xa_wk, xa_wv=xa_wv,
             xa_wo=xa_wo, norm_ffn_pre=norm_ffn_pre, norm_ffn_post=norm_ffn_post,
             ffn_w_up=ffn_w_up, ffn_conv_w=ffn_conv_w, ffn_conv_b=ffn_conv_b, ffn_w_down=ffn_w_down)
    for l in range(w_in.shape[0]):
        x = _layer(x, mem, l, p)
    return x
```

```python
import functools
import math

import numpy as np

import jax
import jax.numpy as jnp
from jax import lax
from jax.experimental import pallas as pl
from jax.experimental.pallas import tpu as pltpu

F32 = jnp.float32
BF16 = jnp.bfloat16

RMS_EPS = 1e-6
SB_HEAD_DIM = 64
SSM_GROUP = 16
SSM_STATE = 64
XA_HEADS = 4
CONV_WIDTH = 3
LANES = 128
SUBLANES = 8
VMEM_LIMIT = 56 * 1024 * 1024
LOG2E = 1.4426950408889634


def _rms(x, g):
    ms = jnp.mean(x * x, axis=-1, keepdims=True)
    return x * lax.rsqrt(ms + RMS_EPS) * g


def _dot(a, b):
    return jnp.dot(a, b, preferred_element_type=F32)


def _gelu_tanh(x):
    a = -2.0 * math.sqrt(2.0 / math.pi) * LOG2E
    return x / (1.0 + jnp.exp2(x * (a + (a * 0.044715) * (x * x))))


def _dot_nt(a, b):
    return lax.dot_general(a, b, (((1,), (1,)), ((), ())), preferred_element_type=F32)


def _params(n_axes):
    return pltpu.CompilerParams(
        dimension_semantics=("arbitrary",) * n_axes, vmem_limit_bytes=VMEM_LIMIT)


def _const_spec(shape):
    zeros = (0,) * len(shape)
    return pl.BlockSpec(shape, lambda *_: zeros, pipeline_mode=pl.Buffered(1))


def _discretize_kernel(are_ref, aim_ref, ldt_ref, bre_ref, bim_ref,
                       lre_ref, lim_ref, bbre_ref, bbim_ref):
    a_re = are_ref[...]
    a_im = aim_ref[...]
    dt = jnp.exp(ldt_ref[...])
    mag = jnp.exp(a_re * dt)
    l_re = mag * jnp.cos(a_im * dt)
    l_im = mag * jnp.sin(a_im * dt)
    lre_ref[...] = l_re
    lim_ref[...] = l_im
    n_re = l_re - 1.0
    inv = 1.0 / (a_re * a_re + a_im * a_im)
    c_re = (n_re * a_re + l_im * a_im) * inv
    c_im = (l_im * a_re - n_re * a_im) * inv
    b_re = bre_ref[...]
    b_im = bim_ref[...]
    bbre_ref[...] = c_re * b_re - c_im * b_im
    bbim_ref[...] = c_re * b_im + c_im * b_re


def _discretize(a_re, a_im, log_dt, b_re, b_im):
    g, p, c = b_re.shape
    rep = lambda a: jnp.repeat(a, c, axis=-1)
    args = (rep(a_re), rep(a_im), jnp.broadcast_to(log_dt[:, None], (g, p * c)),
            b_re.reshape(g, p * c), b_im.reshape(g, p * c))
    out = jax.ShapeDtypeStruct((g, p * c), F32)
    l_re, l_im, bb_re, bb_im = pl.pallas_call(
        _discretize_kernel, out_shape=(out, out, out, out), name="ssm_discretize")(*args)
    return (l_re[:, ::c], l_im[:, ::c], bb_re.reshape(g, p, c), bb_im.reshape(g, p, c))


def _inproj_kernel(x_ref, g_ref, w_ref, bg_ref, qt_ref, k_ref, vt_ref, u_ref, ga_ref, gs_ref,
                   *, sbw, ssw, d, qscale, tq):
    ts = x_ref.shape[0]
    h = _rms(x_ref[...], g_ref[...]).astype(BF16)

    def proj(c0, n):
        return _dot(h, w_ref[:, c0:c0 + n])

    q = proj(0, sbw) * qscale
    v = proj(2 * sbw, sbw)
    for hp in range(sbw // LANES):
        cols = slice(hp * LANES, (hp + 1) * LANES)
        for c in range(ts // tq):
            rows = slice(c * tq, (c + 1) * tq)
            qt_ref[hp, c] = q[rows, cols].T.astype(BF16)
            vt_ref[hp, c] = v[rows, cols].T.astype(BF16)
    k_ref[...] = proj(sbw, sbw).astype(BF16)
    u_ref[...] = proj(3 * sbw, ssw).astype(BF16)
    g0 = 3 * sbw + ssw
    cw = 512
    for c in range(d // cw):
        sl = slice(c * cw, (c + 1) * cw)
        ga_ref[:, sl] = jax.nn.sigmoid(proj(g0 + c * cw, cw) + bg_ref[:, sl]).astype(BF16)
        gs_ref[:, sl] = jax.nn.sigmoid(
            proj(g0 + d + c * cw, cw) + bg_ref[:, d + c * cw:d + (c + 1) * cw]).astype(BF16)


def _in_proj(x, gain, w_in, b_gate, *, sbw, ssw, ts, tq):
    b, s, d = x.shape
    in_w = w_in.shape[1]
    hp = sbw // LANES
    qscale = SB_HEAD_DIM ** -0.5 * LOG2E
    tok = lambda w: pl.BlockSpec((None, ts, w), lambda bi, ti: (bi, ti, 0))
    tr = pl.BlockSpec((None, hp, ts // tq, LANES, tq), lambda bi, ti: (bi, 0, ti, 0, 0))
    out_shape = (
        jax.ShapeDtypeStruct((b, hp, s // tq, LANES, tq), BF16),
        jax.ShapeDtypeStruct((b, s, sbw), BF16),
        jax.ShapeDtypeStruct((b, hp, s // tq, LANES, tq), BF16),
        jax.ShapeDtypeStruct((b, s, ssw), BF16),
        jax.ShapeDtypeStruct((b, s, d), BF16),
        jax.ShapeDtypeStruct((b, s, d), BF16),
    )
    return pl.pallas_call(
        functools.partial(_inproj_kernel, sbw=sbw, ssw=ssw, d=d, qscale=qscale, tq=tq),
        grid=(b, s // ts),
        in_specs=[tok(d), _const_spec((1, d)), _const_spec((d, in_w)), _const_spec((1, 2 * d))],
        out_specs=(tr, tok(sbw), tr, tok(ssw), tok(d), tok(d)),
        out_shape=out_shape,
        compiler_params=_params(2),
        name="in_proj",
    )(x, gain, w_in, b_gate)


_SB_DEPTH = 5
_SB_Z_SLOTS = 4
_SB_SLOTS = 2
_SB_MASK_BIAS = -1e30
_SB_EXP_CAP = 30.0


def _suffix_matrix(nk):
    r = lax.broadcasted_iota(jnp.int32, (nk + SUBLANES, nk), 0)
    c = lax.broadcasted_iota(jnp.int32, (nk + SUBLANES, nk), 1)
    return ((c >= r) | (r >= nk)).astype(BF16)


def _sb_item_table(nq):
    phases = [[(qi, qi) for qi in range(nq)],
              [(qi, qi - j) for qi in range(nq) for j in range(1, qi + 1)]]
    cols, bounds = [], [0]
    for items in phases:
        n = len(items)
        n_steps = -(-(n + _SB_DEPTH - 1) // _SB_Z_SLOTS) * _SB_Z_SLOTS if n else 0
        for t in range(n_steps):
            a = items[min(t, n - 1)]
            c = items[t - 3][0] if 0 <= t - 3 < n else nq
            d = items[t - 4] if 0 <= t - 4 < n else (nq, 0)
            cols.append((a[0], a[1], c, d[0], d[1]))
        bounds.append(len(cols))
    return np.asarray(cols, np.int32).T.copy(), bounds


def _sb_attn_kernel(tab_ref, qt_ref, k_ref, vt_ref, u_ref, bias_ref, o_ref,
                    z_ref, sp_ref, ct_ref, w_ref, acc_ref, r_ref, *, tq, bounds):
    nq = qt_ref.shape[0]
    drow = lax.broadcasted_iota(jnp.int32, (LANES, tq), 0)
    head_rows = [(drow // SB_HEAD_DIM) == h for h in range(2)]

    for ref in (z_ref, sp_ref, ct_ref, w_ref, acc_ref, r_ref):
        ref[...] = jnp.zeros_like(ref)

    def m1(t, zs, diagonal):
        qt = qt_ref[tab_ref[0, t]]
        ks = pl.multiple_of(tab_ref[1, t] * tq, tq)
        kc = k_ref[pl.ds(ks, tq), :]
        for h in range(2):
            z = _dot(kc, jnp.where(head_rows[h], qt, jnp.zeros_like(qt)))
            z_ref[zs, h] = z + bias_ref[...] if diagonal else z

    def v1(zs, s):
        for h in range(2):
            z = z_ref[zs, h]
            lg = jnp.log(1.0 + jnp.exp2(jnp.minimum(z, _SB_EXP_CAP))) * LOG2E
            sp_ref[s, h] = jnp.maximum(z, lg).astype(BF16)

    def m2(s_in, s_out):
        for h in range(2):
            ct_ref[s_out, h] = _dot(u_ref[...], sp_ref[s_in, h])

    def v2(t, zs, s_in, s_out, diagonal):
        qi = tab_ref[2, t]
        for h in range(2):
            total = ct_ref[s_in, h, tq:, :]
            if diagonal:
                arg = z_ref[zs, h] - ct_ref[s_in, h, :tq, :]
                r_ref[qi, h] = total
            else:
                r = r_ref[qi, h]
                arg = z_ref[zs, h] - ct_ref[s_in, h, :tq, :] - r[0:1, :]
                r_ref[qi, h] = r + total
            w_ref[s_out, h] = jnp.exp2(arg).astype(BF16)

    def m3(t, s_in, diagonal):
        qi = tab_ref[3, t]
        vc = vt_ref[tab_ref[4, t]]
        for h in range(2):
            pv = _dot(vc[h * SB_HEAD_DIM:(h + 1) * SB_HEAD_DIM, :], w_ref[s_in, h])
            acc_ref[qi, h] = pv if diagonal else acc_ref[qi, h] + pv

    def run_phase(first_step, last_step, diagonal):
        def outer(i, carry):
            for u in range(_SB_Z_SLOTS):
                t = i * _SB_Z_SLOTS + u
                cur, prev = u % _SB_SLOTS, (u - 1) % _SB_SLOTS
                m3(t, prev, diagonal)
                v2(t, (u - 3) % _SB_Z_SLOTS, prev, cur, diagonal)
                m2(prev, cur)
                v1((u - 1) % _SB_Z_SLOTS, cur)
                m1(t, u, diagonal)
            return carry

        lax.fori_loop(first_step // _SB_Z_SLOTS, last_step // _SB_Z_SLOTS, outer, 0)

    run_phase(bounds[0], bounds[1], True)
    run_phase(bounds[1], bounds[2], False)
    for qi in range(nq):
        acc = jnp.concatenate([acc_ref[qi, 0], acc_ref[qi, 1]], axis=0)
        o_ref[qi * tq:(qi + 1) * tq, :] = acc.T.astype(o_ref.dtype)


def _sb_attn(qt, k, vt, *, tq):
    b, hp, nq, _, _ = qt.shape
    s = k.shape[1]
    tab, bounds = _sb_item_table(nq)
    u_mat = _suffix_matrix(tq)
    krow = lax.broadcasted_iota(jnp.int32, (tq, tq), 0)
    qcol = lax.broadcasted_iota(jnp.int32, (tq, tq), 1)
    bias = jnp.where(krow < qcol, 0.0, _SB_MASK_BIAS).astype(F32)
    tr = pl.BlockSpec((None, None, nq, LANES, tq), lambda bi, hi, tab: (bi, hi, 0, 0, 0))
    tokb = pl.BlockSpec((None, s, LANES), lambda bi, hi, tab: (bi, 0, hi))
    const = lambda a: pl.BlockSpec(a.shape, lambda bi, hi, tab: (0, 0), pipeline_mode=pl.Buffered(1))
    grid_spec = pltpu.PrefetchScalarGridSpec(
        num_scalar_prefetch=1,
        grid=(b, hp),
        in_specs=[tr, tokb, tr, const(u_mat), const(bias)],
        out_specs=tokb,
        scratch_shapes=[
            pltpu.VMEM((_SB_Z_SLOTS, 2, tq, tq), F32),
            pltpu.VMEM((_SB_SLOTS, 2, tq, tq), BF16),
            pltpu.VMEM((_SB_SLOTS, 2, tq + SUBLANES, tq), F32),
            pltpu.VMEM((_SB_SLOTS, 2, tq, tq), BF16),
            pltpu.VMEM((nq + 1, 2, SB_HEAD_DIM, tq), F32),
            pltpu.VMEM((nq + 1, 2, SUBLANES, tq), F32),
        ],
    )
    return pl.pallas_call(
        functools.partial(_sb_attn_kernel, tq=tq, bounds=tuple(bounds)),
        grid_spec=grid_spec,
        out_shape=jax.ShapeDtypeStruct((b, s, hp * LANES), BF16),
        compiler_params=_params(2),
        name="sb_attn",
    )(jnp.asarray(tab), qt, k, vt, u_mat, bias)


def _ssm_kernel(u_ref, perm_ref, permt_ref, bs_ref, are_ref, aim_ref, cs_ref, d_ref, wg_ref, bg_ref,
                o_ref, bu_ref, xs_ref, *, t_steps, n_state):
    nb = u_ref.shape[0]
    n_slab = u_ref.shape[2] // LANES
    sw = n_state // n_slab

    @pl.when(pl.program_id(0) == 0)
    def _():
        xs_ref[...] = jnp.zeros_like(xs_ref)

    u_bt = u_ref[...].reshape(nb * t_steps, u_ref.shape[2])
    u = _dot(perm_ref[...], u_bt)
    ub = u.astype(BF16)
    for s in range(n_slab):
        bu = _dot(ub[:, s * LANES:(s + 1) * LANES], bs_ref[s])
        bu_ref[:, s * sw:(s + 1) * sw] = bu[:, :sw]
        bu_ref[:, n_state + s * sw:n_state + (s + 1) * sw] = bu[:, sw:]

    n_part = 2
    pw = n_state // n_part
    for part in range(n_part):
        re = slice(part * pw, (part + 1) * pw)
        im = slice(n_state + part * pw, n_state + (part + 1) * pw)
        a_re = are_ref[:, re]
        a_im = aim_ref[:, re]

        def body(t, carry, re=re, im=im, a_re=a_re, a_im=a_im):
            x_re, x_im = carry
            rows = pl.ds(pl.multiple_of(t * nb, nb), nb)
            n_re = a_re * x_re - a_im * x_im + bu_ref[rows, re]
            n_im = a_re * x_im + a_im * x_re + bu_ref[rows, im]
            bu_ref[rows, re] = n_re
            bu_ref[rows, im] = n_im
            return n_re, n_im

        x_re, x_im = lax.fori_loop(0, t_steps, body, (xs_ref[:, re], xs_ref[:, im]), unroll=2)
        xs_ref[:, re] = x_re
        xs_ref[:, im] = x_im

    ys = []
    for s in range(n_slab):
        x_re = bu_ref[:, s * sw:(s + 1) * sw].astype(BF16)
        x_im = bu_ref[:, n_state + s * sw:n_state + (s + 1) * sw].astype(BF16)
        ys.append(_dot(x_re, cs_ref[s, :sw, :]) + _dot(x_im, cs_ref[s, sw:, :]))
    y = jnp.concatenate(ys, axis=1) + d_ref[...] * u
    y = _gelu_tanh(y)
    gate = jax.nn.sigmoid(_dot(y.astype(BF16), wg_ref[...]) + bg_ref[...])
    out = _dot(permt_ref[...], (y * gate).astype(BF16))
    o_ref[...] = out.astype(o_ref.dtype).reshape(o_ref.shape)


def _ssm(u, b_slab, lam_re, lam_im, c_slab, d_skip, w_glu, b_glu, *, t_steps):
    batch, s, ssw = u.shape
    n_state = lam_re.shape[1]
    assert batch == SUBLANES, "the scan puts the batch on the sublanes"
    blk = t_steps * batch
    r = np.arange(blk)
    perm = np.zeros((blk, blk), np.float32)
    perm[r, (r % batch) * t_steps + r // batch] = 1.0
    perm = jnp.asarray(perm, BF16)
    tblk = pl.BlockSpec((batch, t_steps, ssw), lambda i: (0, i, 0))
    return pl.pallas_call(
        functools.partial(_ssm_kernel, t_steps=t_steps, n_state=n_state),
        grid=(s // t_steps,),
        in_specs=[
            tblk, _const_spec(perm.shape), _const_spec(perm.shape),
            _const_spec(b_slab.shape), _const_spec(lam_re.shape), _const_spec(lam_im.shape),
            _const_spec(c_slab.shape), _const_spec(d_skip.shape), _const_spec(w_glu.shape),
            _const_spec(b_glu.shape),
        ],
        out_specs=tblk,
        out_shape=jax.ShapeDtypeStruct((batch, s, ssw), BF16),
        scratch_shapes=[pltpu.VMEM((blk, 2 * n_state), F32),
                        pltpu.VMEM((batch, 2 * n_state), F32)],
        compiler_params=_params(1),
        name="ssm",
    )(u, perm, perm.T, b_slab, lam_re, lam_im, c_slab, d_skip, w_glu, b_glu)


def _merge_kernel(oa_ref, os_ref, ga_ref, gs_ref, x_ref, wa_ref, ws_ref, wo_ref, g_ref, out_ref):
    pa = _dot(oa_ref[...], wa_ref[...])
    ps = _dot(os_ref[...], ws_ref[...])
    merged = ga_ref[...].astype(F32) * pa + gs_ref[...].astype(F32) * ps
    y = _dot(merged.astype(BF16), wo_ref[...])
    out_ref[...] = x_ref[...] + _rms(y, g_ref[...])


def _merge_out(o_attn, o_ssm, ga, gs, x, wa, ws, wo, gain, *, ts):
    b, s, d = x.shape
    sbw = o_attn.shape[2]
    ssw = o_ssm.shape[2]
    tok = lambda w: pl.BlockSpec((None, ts, w), lambda bi, ti: (bi, ti, 0))
    return pl.pallas_call(
        _merge_kernel,
        grid=(b, s // ts),
        in_specs=[tok(sbw), tok(ssw), tok(d), tok(d), tok(d),
                  _const_spec(wa.shape), _const_spec(ws.shape), _const_spec(wo.shape),
                  _const_spec(gain.shape)],
        out_specs=tok(d),
        out_shape=jax.ShapeDtypeStruct((b, s, d), F32),
        compiler_params=_params(2),
        name="merge_out",
    )(o_attn, o_ssm, ga, gs, x, wa, ws, wo, gain)


def _memkv_kernel(m_ref, g_ref, wk_ref, wv_ref, k_ref, v_ref):
    mn = _rms(m_ref[...], g_ref[...]).astype(BF16)
    k_ref[...] = _dot(mn, wk_ref[...]).astype(BF16)
    v_ref[...] = _dot(mn, wv_ref[...]).astype(BF16)


def _mem_kv(mem, gain, wk, wv):
    b, m, d = mem.shape
    blk = pl.BlockSpec((None, m, d), lambda bi: (bi, 0, 0))
    out = jax.ShapeDtypeStruct((b, m, d), BF16)
    return pl.pallas_call(
        _memkv_kernel,
        grid=(b,),
        in_specs=[blk, _const_spec(gain.shape), _const_spec(wk.shape), _const_spec(wv.shape)],
        out_specs=(blk, blk),
        out_shape=(out, out),
        compiler_params=_params(1),
        name="mem_kv",
    )(mem, gain, wk, wv)


def _xattn_kernel(x_ref, gpre_ref, wq_ref, k_ref, v_ref, wo_ref, gpost_ref, out_ref, *, heads):
    x = x_ref[...]
    d = x.shape[1]
    hd = d // heads
    h = _rms(x, gpre_ref[...]).astype(BF16)
    q = (_dot(h, wq_ref[...]) * (hd ** -0.5)).astype(BF16)
    outs = []
    for i in range(heads):
        sl = slice(i * hd, (i + 1) * hd)
        s = _dot_nt(q[:, sl], k_ref[:, sl])
        p = jnp.exp(s - jnp.max(s, axis=-1, keepdims=True))
        inv = 1.0 / jnp.sum(p, axis=-1, keepdims=True)
        outs.append((_dot(p.astype(BF16), v_ref[:, sl]) * inv).astype(BF16))
    y = _dot(jnp.concatenate(outs, axis=1), wo_ref[...])
    out_ref[...] = x + _rms(y, gpost_ref[...])


def _xattn(x, gpre, wq, kx, vx, wo, gpost, *, ts):
    b, s, d = x.shape
    m = kx.shape[1]
    tok = pl.BlockSpec((None, ts, d), lambda bi, ti: (bi, ti, 0))
    memb = pl.BlockSpec((None, m, d), lambda bi, ti: (bi, 0, 0))
    return pl.pallas_call(
        functools.partial(_xattn_kernel, heads=XA_HEADS),
        grid=(b, s // ts),
        in_specs=[tok, _const_spec(gpre.shape), _const_spec(wq.shape), memb, memb,
                  _const_spec(wo.shape), _const_spec(gpost.shape)],
        out_specs=tok,
        out_shape=jax.ShapeDtypeStruct((b, s, d), F32),
        compiler_params=_params(2),
        name="xattn",
    )(x, gpre, wq, kx, vx, wo, gpost)


def _ffn_kernel(x_ref, gpre_ref, wup_ref, cw_ref, cb_ref, wdn_ref, gpost_ref, out_ref,
                carry_ref, acc_ref, h_ref, up_ref, a_ref, *, d_ff, cwid):
    ts = x_ref.shape[0]
    halo = SUBLANES
    n_chunk = d_ff // cwid

    @pl.when(pl.program_id(1) == 0)
    def _():
        carry_ref[...] = jnp.zeros_like(carry_ref)

    h_ref[...] = _rms(x_ref[...], gpre_ref[...]).astype(BF16)

    def stage_up(c, slot):
        for half in range(2):
            c0 = half * d_ff + c * cwid
            up_ref[slot, half] = _dot(h_ref[...], wup_ref[:, c0:c0 + cwid])

    def conv(c0, up):
        ext = jnp.concatenate([carry_ref[:, c0:c0 + cwid], up], axis=0)
        carry_ref[:, c0:c0 + cwid] = up[ts - halo:, :]
        out = cb_ref[:, c0:c0 + cwid] + cw_ref[CONV_WIDTH - 1:CONV_WIDTH, c0:c0 + cwid] * up
        for back in range(1, CONV_WIDTH):
            tap = pltpu.roll(ext, back, 0)[halo:, :]
            out = out + cw_ref[CONV_WIDTH - 1 - back:CONV_WIDTH - back, c0:c0 + cwid] * tap
        return out

    def stage_act(c, slot):
        gate = conv(c * cwid, up_ref[slot, 0])
        val = conv(d_ff + c * cwid, up_ref[slot, 1])
        a_ref[slot] = (_gelu_tanh(gate) * val).astype(BF16)

    def stage_down(c, slot):
        part = _dot(a_ref[slot], wdn_ref[c * cwid:(c + 1) * cwid, :])
        acc_ref[...] = part if c == 0 else acc_ref[...] + part

    for i in range(n_chunk + 2):
        if i < n_chunk:
            stage_up(i, i % 2)
        if i >= 2:
            stage_down(i - 2, i % 2)
        if 1 <= i <= n_chunk:
            stage_act(i - 1, (i - 1) % 2)
    out_ref[...] = x_ref[...] + _rms(acc_ref[...], gpost_ref[...])


def _conv_ffn(x, gpre, w_up, conv_w, conv_b, w_dn, gpost, *, ts, cwid):
    b, s, d = x.shape
    d_ff = w_dn.shape[0]
    tok = pl.BlockSpec((None, ts, d), lambda bi, ti: (bi, ti, 0))
    return pl.pallas_call(
        functools.partial(_ffn_kernel, d_ff=d_ff, cwid=cwid),
        grid=(b, s // ts),
        in_specs=[tok, _const_spec(gpre.shape), _const_spec(w_up.shape), _const_spec(conv_w.shape),
                  _const_spec(conv_b.shape), _const_spec(w_dn.shape), _const_spec(gpost.shape)],
        out_specs=tok,
        out_shape=jax.ShapeDtypeStruct((b, s, d), F32),
        scratch_shapes=[pltpu.VMEM((SUBLANES, 2 * d_ff), F32),
                        pltpu.VMEM((ts, d), F32),
                        pltpu.VMEM((ts, d), BF16),
                        pltpu.VMEM((2, 2, ts, cwid), F32),
                        pltpu.VMEM((2, ts, cwid), BF16)],
        compiler_params=_params(2),
        name="conv_ffn",
    )(x, gpre, w_up, conv_w, conv_b, w_dn, gpost)


def _ssm_slabs(bb_re, bb_im, c_re, c_im):
    g, p, c = bb_re.shape
    gps = LANES // c
    n_slab = g // gps
    eye = jnp.eye(gps, dtype=F32)

    def b_part(bb):
        t = bb.reshape(n_slab, gps, p, c)
        return jnp.einsum('sgpc,gh->sgchp', t, eye).reshape(n_slab, gps * c, gps * p)

    def c_part(cc):
        t = cc.reshape(n_slab, gps, c, p)
        return jnp.einsum('sgcp,gh->shpgc', t, eye).reshape(n_slab, gps * p, gps * c)

    b_slab = jnp.concatenate([b_part(bb_re), b_part(bb_im)], axis=2).astype(BF16)
    c_slab = jnp.concatenate([c_part(c_re), c_part(-c_im)], axis=1).astype(BF16)
    return b_slab, c_slab


def _layer(x, mem, l, p):
    b, s, d = x.shape
    sbw = p["w_branch_attn"].shape[1]
    ssw = p["w_branch_ssm"].shape[1]
    row = lambda a: a[l][None, :].astype(F32)
    wb = lambda a: a[l].astype(BF16)

    lam_re, lam_im, bb_re, bb_im = _discretize(
        p["ssm_a_re"][l], p["ssm_a_im"][l], p["ssm_log_dt"][l], p["ssm_b_re"][l], p["ssm_b_im"][l])
    b_slab, c_slab = _ssm_slabs(bb_re, bb_im, p["ssm_c_re"][l].astype(F32), p["ssm_c_im"][l].astype(F32))
    n_state = lam_re.size
    lam_re_b = jnp.broadcast_to(lam_re.reshape(1, n_state), (SUBLANES, n_state))
    lam_im_b = jnp.broadcast_to(lam_im.reshape(1, n_state), (SUBLANES, n_state))

    qt, k, vt, u, ga, gs = _in_proj(x, row(p["norm_mix_pre"]), wb(p["w_in"]), row(p["b_gate"]),
                                       sbw=sbw, ssw=ssw, ts=512, tq=256)
    o_attn = _sb_attn(qt, k, vt, tq=256)
    o_ssm = _ssm(u, b_slab, lam_re_b, lam_im_b, c_slab,
                 row(p["ssm_d"]), wb(p["ssm_w_glu"]), row(p["ssm_b_glu"]), t_steps=64)
    x = _merge_out(o_attn, o_ssm, ga, gs, x, wb(p["w_branch_attn"]),
                   wb(p["w_branch_ssm"]), wb(p["w_out"]), row(p["norm_mix_post"]), ts=512)

    kx, vx = _mem_kv(mem, row(p["norm_mem"]), wb(p["xa_wk"]), wb(p["xa_wv"]))
    x = _xattn(x, row(p["norm_xa_pre"]), wb(p["xa_wq"]), kx, vx, wb(p["xa_wo"]),
               row(p["norm_xa_post"]), ts=512)

    x = _conv_ffn(x, row(p["norm_ffn_pre"]), wb(p["ffn_w_up"]), p["ffn_conv_w"][l].astype(F32),
                  row(p["ffn_conv_b"]), wb(p["ffn_w_down"]), row(p["norm_ffn_post"]), ts=512, cwid=256)
    return x


def kernel(x, mem, norm_mix_pre, norm_mix_post, w_in, b_gate, ssm_a_re, ssm_a_im, ssm_log_dt, ssm_b_re, ssm_b_im, ssm_c_re, ssm_c_im, ssm_d, ssm_w_glu, ssm_b_glu, w_branch_attn, w_branch_ssm, w_out, norm_xa_pre, norm_xa_post, norm_mem, xa_wq, xa_wk, xa_wv, xa_wo, norm_ffn_pre, norm_ffn_post, ffn_w_up, ffn_conv_w, ffn_conv_b, ffn_w_down):
    p = dict(norm_mix_pre=norm_mix_pre, norm_mix_post=norm_mix_post, w_in=w_in, b_gate=b_gate,
             ssm_a_re=ssm_a_re, ssm_a_im=ssm_a_im, ssm_log_dt=ssm_log_dt, ssm_b_re=ssm_b_re,
             ssm_b_im=ssm_b_im, ssm_c_re=ssm_c_re, ssm_c_im=ssm_c_im, ssm_d=ssm_d,
             ssm_w_glu=ssm_w_glu, ssm_b_glu=ssm_b_glu, w_branch_attn=w_branch_attn,
             w_branch_ssm=w_branch_ssm, w_out=w_out, norm_xa_pre=norm_xa_pre,
             norm_xa_post=norm_xa_post, norm_mem=norm_mem, xa_wq=xa_wq, xa_wk=xa_wk, xa_wv=xa_wv,
             xa_wo=xa_wo, norm_ffn_pre=norm_ffn_pre, norm_ffn_post=norm_ffn_post,
             ffn_w_up=ffn_w_up, ffn_conv_w=ffn_conv_w, ffn_conv_b=ffn_conv_b, ffn_w_down=ffn_w_down)
    for l in range(w_in.shape[0]):
        x = _layer(x, mem, l, p)
    return x
```

```python
import functools
import math

import numpy as np

import jax
import jax.numpy as jnp
from jax import lax
from jax.experimental import pallas as pl
from jax.experimental.pallas import tpu as pltpu

F32 = jnp.float32
BF16 = jnp.bfloat16

RMS_EPS = 1e-6
SB_HEAD_DIM = 64
SSM_GROUP = 16
SSM_STATE = 64
XA_HEADS = 4
CONV_WIDTH = 3
LANES = 128
SUBLANES = 8
VMEM_LIMIT = 56 * 1024 * 1024
LOG2E = 1.4426950408889634


def _rms(x, g):
    ms = jnp.mean(x * x, axis=-1, keepdims=True)
    return x * lax.rsqrt(ms + RMS_EPS) * g


def _dot(a, b):
    return jnp.dot(a, b, preferred_element_type=F32)


def _gelu_tanh(x):
    a = -2.0 * math.sqrt(2.0 / math.pi) * LOG2E
    return x / (1.0 + jnp.exp2(x * (a + (a * 0.044715) * (x * x))))


def _dot_nt(a, b):
    return lax.dot_general(a, b, (((1,), (1,)), ((), ())), preferred_element_type=F32)


def _params(n_axes):
    return pltpu.CompilerParams(
        dimension_semantics=("arbitrary",) * n_axes, vmem_limit_bytes=VMEM_LIMIT)


def _const_spec(shape):
    zeros = (0,) * len(shape)
    return pl.BlockSpec(shape, lambda *_: zeros, pipeline_mode=pl.Buffered(1))


def _discretize_kernel(are_ref, aim_ref, ldt_ref, bre_ref, bim_ref,
                       lre_ref, lim_ref, bbre_ref, bbim_ref):
    a_re = are_ref[...]
    a_im = aim_ref[...]
    dt = jnp.exp(ldt_ref[...])
    mag = jnp.exp(a_re * dt)
    l_re = mag * jnp.cos(a_im * dt)
    l_im = mag * jnp.sin(a_im * dt)
    lre_ref[...] = l_re
    lim_ref[...] = l_im
    n_re = l_re - 1.0
    inv = 1.0 / (a_re * a_re + a_im * a_im)
    c_re = (n_re * a_re + l_im * a_im) * inv
    c_im = (l_im * a_re - n_re * a_im) * inv
    b_re = bre_ref[...]
    b_im = bim_ref[...]
    bbre_ref[...] = c_re * b_re - c_im * b_im
    bbim_ref[...] = c_re * b_im + c_im * b_re


def _discretize(a_re, a_im, log_dt, b_re, b_im):
    g, p, c = b_re.shape
    rep = lambda a: jnp.repeat(a, c, axis=-1)
    args = (rep(a_re), rep(a_im), jnp.broadcast_to(log_dt[:, None], (g, p * c)),
            b_re.reshape(g, p * c), b_im.reshape(g, p * c))
    out = jax.ShapeDtypeStruct((g, p * c), F32)
    l_re, l_im, bb_re, bb_im = pl.pallas_call(
        _discretize_kernel, out_shape=(out, out, out, out), name="ssm_discretize")(*args)
    return (l_re[:, ::c], l_im[:, ::c], bb_re.reshape(g, p, c), bb_im.reshape(g, p, c))


def _inproj_kernel(x_ref, g_ref, w_ref, bg_ref, qt_ref, k_ref, vt_ref, u_ref, ga_ref, gs_ref,
                   *, sbw, ssw, d, qscale, tq):
    ts = x_ref.shape[0]
    h = _rms(x_ref[...], g_ref[...]).astype(BF16)

    def proj(c0, n):
        return _dot(h, w_ref[:, c0:c0 + n])

    q = proj(0, sbw) * qscale
    v = proj(2 * sbw, sbw)
    for hp in range(sbw // LANES):
        cols = slice(hp * LANES, (hp + 1) * LANES)
        for c in range(ts // tq):
            rows = slice(c * tq, (c + 1) * tq)
            qt_ref[hp, c] = q[rows, cols].T.astype(BF16)
            vt_ref[hp, c] = v[rows, cols].T.astype(BF16)
    k_ref[...] = proj(sbw, sbw).astype(BF16)
    u_ref[...] = proj(3 * sbw, ssw).astype(BF16)
    g0 = 3 * sbw + ssw
    cw = 512
    for c in range(d // cw):
        sl = slice(c * cw, (c + 1) * cw)
        ga_ref[:, sl] = jax.nn.sigmoid(proj(g0 + c * cw, cw) + bg_ref[:, sl]).astype(BF16)
        gs_ref[:, sl] = jax.nn.sigmoid(
            proj(g0 + d + c * cw, cw) + bg_ref[:, d + c * cw:d + (c + 1) * cw]).astype(BF16)


def _in_proj(x, gain, w_in, b_gate, *, sbw, ssw, ts, tq):
    b, s, d = x.shape
    in_w = w_in.shape[1]
    hp = sbw // LANES
    qscale = SB_HEAD_DIM ** -0.5 * LOG2E
    tok = lambda w: pl.BlockSpec((None, ts, w), lambda bi, ti: (bi, ti, 0))
    tr = pl.BlockSpec((None, hp, ts // tq, LANES, tq), lambda bi, ti: (bi, 0, ti, 0, 0))
    out_shape = (
        jax.ShapeDtypeStruct((b, hp, s // tq, LANES, tq), BF16),
        jax.ShapeDtypeStruct((b, s, sbw), BF16),
        jax.ShapeDtypeStruct((b, hp, s // tq, LANES, tq), BF16),
        jax.ShapeDtypeStruct((b, s, ssw), BF16),
        jax.ShapeDtypeStruct((b, s, d), BF16),
        jax.ShapeDtypeStruct((b, s, d), BF16),
    )
    return pl.pallas_call(
        functools.partial(_inproj_kernel, sbw=sbw, ssw=ssw, d=d, qscale=qscale, tq=tq),
        grid=(b, s // ts),
        in_specs=[tok(d), _const_spec((1, d)), _const_spec((d, in_w)), _const_spec((1, 2 * d))],
        out_specs=(tr, tok(sbw), tr, tok(ssw), tok(d), tok(d)),
        out_shape=out_shape,
        compiler_params=_params(2),
        name="in_proj",
    )(x, gain, w_in, b_gate)


_SB_DEPTH = 5
_SB_Z_SLOTS = 4
_SB_SLOTS = 2
_SB_MASK_BIAS = -1e30
_SB_EXP_CAP = 30.0


def _suffix_matrix(nk):
    r = lax.broadcasted_iota(jnp.int32, (nk + SUBLANES, nk), 0)
    c = lax.broadcasted_iota(jnp.int32, (nk + SUBLANES, nk), 1)
    return ((c >= r) | (r >= nk)).astype(BF16)


def _sb_item_table(nq):
    phases = [[(qi, qi) for qi in range(nq)],
              [(qi, qi - j) for qi in range(nq) for j in range(1, qi + 1)]]
    cols, bounds = [], [0]
    for items in phases:
        n = len(items)
        n_steps = -(-(n + _SB_DEPTH - 1) // _SB_Z_SLOTS) * _SB_Z_SLOTS if n else 0
        for t in range(n_steps):
            a = items[min(t, n - 1)]
            c = items[t - 3][0] if 0 <= t - 3 < n else nq
            d = items[t - 4] if 0 <= t - 4 < n else (nq, 0)
            cols.append((a[0], a[1], c, d[0], d[1]))
        bounds.append(len(cols))
    return np.asarray(cols, np.int32).T.copy(), bounds


def _sb_attn_kernel(tab_ref, qt_ref, k_ref, vt_ref, u_ref, bias_ref, o_ref,
                    z_ref, sp_ref, ct_ref, w_ref, acc_ref, r_ref, *, tq, bounds):
    nq = qt_ref.shape[0]
    drow = lax.broadcasted_iota(jnp.int32, (LANES, tq), 0)
    head_rows = [(drow // SB_HEAD_DIM) == h for h in range(2)]

    for ref in (z_ref, sp_ref, ct_ref, w_ref, acc_ref, r_ref):
        ref[...] = jnp.zeros_like(ref)

    def m1(t, zs, diagonal):
        qt = qt_ref[tab_ref[0, t]]
        ks = pl.multiple_of(tab_ref[1, t] * tq, tq)
        kc = k_ref[pl.ds(ks, tq), :]
        for h in range(2):
            z = _dot(kc, jnp.where(head_rows[h], qt, jnp.zeros_like(qt)))
            z_ref[zs, h] = z + bias_ref[...] if diagonal else z

    def v1(zs, s):
        for h in range(2):
            z = z_ref[zs, h]
            lg = jnp.log(1.0 + jnp.exp2(jnp.minimum(z, _SB_EXP_CAP))) * LOG2E
            sp_ref[s, h] = jnp.maximum(z, lg).astype(BF16)

    def m2(s_in, s_out):
        for h in range(2):
            ct_ref[s_out, h] = _dot(u_ref[...], sp_ref[s_in, h])

    def v2(t, zs, s_in, s_out, diagonal):
        qi = tab_ref[2, t]
        for h in range(2):
            total = ct_ref[s_in, h, tq:, :]
            if diagonal:
                arg = z_ref[zs, h] - ct_ref[s_in, h, :tq, :]
                r_ref[qi, h] = total
            else:
                r = r_ref[qi, h]
                arg = z_ref[zs, h] - ct_ref[s_in, h, :tq, :] - r[0:1, :]
                r_ref[qi, h] = r + total
            w_ref[s_out, h] = jnp.exp2(arg).astype(BF16)

    def m3(t, s_in, diagonal):
        qi = tab_ref[3, t]
        vc = vt_ref[tab_ref[4, t]]
        for h in range(2):
            pv = _dot(vc[h * SB_HEAD_DIM:(h + 1) * SB_HEAD_DIM, :], w_ref[s_in, h])
            acc_ref[qi, h] = pv if diagonal else acc_ref[qi, h] + pv

    def run_phase(first_step, last_step, diagonal):
        def outer(i, carry):
            for u in range(_SB_Z_SLOTS):
                t = i * _SB_Z_SLOTS + u
                cur, prev = u % _SB_SLOTS, (u - 1) % _SB_SLOTS
                m3(t, prev, diagonal)
                v2(t, (u - 3) % _SB_Z_SLOTS, prev, cur, diagonal)
                m2(prev, cur)
                v1((u - 1) % _SB_Z_SLOTS, cur)
                m1(t, u, diagonal)
            return carry

        lax.fori_loop(first_step // _SB_Z_SLOTS, last_step // _SB_Z_SLOTS, outer, 0)

    run_phase(bounds[0], bounds[1], True)
    run_phase(bounds[1], bounds[2], False)
    for qi in range(nq):
        acc = jnp.concatenate([acc_ref[qi, 0], acc_ref[qi, 1]], axis=0)
        o_ref[qi * tq:(qi + 1) * tq, :] = acc.T.astype(o_ref.dtype)


def _sb_attn(qt, k, vt, *, tq):
    b, hp, nq, _, _ = qt.shape
    s = k.shape[1]
    tab, bounds = _sb_item_table(nq)
    u_mat = _suffix_matrix(tq)
    krow = lax.broadcasted_iota(jnp.int32, (tq, tq), 0)
    qcol = lax.broadcasted_iota(jnp.int32, (tq, tq), 1)
    bias = jnp.where(krow < qcol, 0.0, _SB_MASK_BIAS).astype(F32)
    tr = pl.BlockSpec((None, None, nq, LANES, tq), lambda bi, hi, tab: (bi, hi, 0, 0, 0))
    tokb = pl.BlockSpec((None, s, LANES), lambda bi, hi, tab: (bi, 0, hi))
    const = lambda a: pl.BlockSpec(a.shape, lambda bi, hi, tab: (0, 0), pipeline_mode=pl.Buffered(1))
    grid_spec = pltpu.PrefetchScalarGridSpec(
        num_scalar_prefetch=1,
        grid=(b, hp),
        in_specs=[tr, tokb, tr, const(u_mat), const(bias)],
        out_specs=tokb,
        scratch_shapes=[
            pltpu.VMEM((_SB_Z_SLOTS, 2, tq, tq), F32),
            pltpu.VMEM((_SB_SLOTS, 2, tq, tq), BF16),
            pltpu.VMEM((_SB_SLOTS, 2, tq + SUBLANES, tq), F32),
            pltpu.VMEM((_SB_SLOTS, 2, tq, tq), BF16),
            pltpu.VMEM((nq + 1, 2, SB_HEAD_DIM, tq), F32),
            pltpu.VMEM((nq + 1, 2, SUBLANES, tq), F32),
        ],
    )
    return pl.pallas_call(
        functools.partial(_sb_attn_kernel, tq=tq, bounds=tuple(bounds)),
        grid_spec=grid_spec,
        out_shape=jax.ShapeDtypeStruct((b, s, hp * LANES), BF16),
        compiler_params=_params(2),
        name="sb_attn",
    )(jnp.asarray(tab), qt, k, vt, u_mat, bias)


def _ssm_kernel(u_ref, perm_ref, permt_ref, bs_ref, are_ref, aim_ref, cs_ref, d_ref, wg_ref, bg_ref,
                o_ref, bu_ref, xs_ref, *, t_steps, n_state):
    nb = u_ref.shape[0]
    n_slab = u_ref.shape[2] // LANES
    sw = n_state // n_slab

    @pl.when(pl.program_id(0) == 0)
    def _():
        xs_ref[...] = jnp.zeros_like(xs_ref)

    u_bt = u_ref[...].reshape(nb * t_steps, u_ref.shape[2])
    u = _dot(perm_ref[...], u_bt)
    ub = u.astype(BF16)
    for s in range(n_slab):
        bu = _dot(ub[:, s * LANES:(s + 1) * LANES], bs_ref[s])
        bu_ref[:, s * sw:(s + 1) * sw] = bu[:, :sw]
        bu_ref[:, n_state + s * sw:n_state + (s + 1) * sw] = bu[:, sw:]

    n_part = 2
    pw = n_state // n_part
    for part in range(n_part):
        re = slice(part * pw, (part + 1) * pw)
        im = slice(n_state + part * pw, n_state + (part + 1) * pw)
        a_re = are_ref[:, re]
        a_im = aim_ref[:, re]

        def body(t, carry, re=re, im=im, a_re=a_re, a_im=a_im):
            x_re, x_im = carry
            rows = pl.ds(pl.multiple_of(t * nb, nb), nb)
            n_re = a_re * x_re - a_im * x_im + bu_ref[rows, re]
            n_im = a_re * x_im + a_im * x_re + bu_ref[rows, im]
            bu_ref[rows, re] = n_re
            bu_ref[rows, im] = n_im
            return n_re, n_im

        x_re, x_im = lax.fori_loop(0, t_steps, body, (xs_ref[:, re], xs_ref[:, im]), unroll=2)
        xs_ref[:, re] = x_re
        xs_ref[:, im] = x_im

    ys = []
    for s in range(n_slab):
        x_re = bu_ref[:, s * sw:(s + 1) * sw].astype(BF16)
        x_im = bu_ref[:, n_state + s * sw:n_state + (s + 1) * sw].astype(BF16)
        ys.append(_dot(x_re, cs_ref[s, :sw, :]) + _dot(x_im, cs_ref[s, sw:, :]))
    y = jnp.concatenate(ys, axis=1) + d_ref[...] * u
    y = _gelu_tanh(y)
    gate = jax.nn.sigmoid(_dot(y.astype(BF16), wg_ref[...]) + bg_ref[...])
    out = _dot(permt_ref[...], (y * gate).astype(BF16))
    o_ref[...] = out.astype(o_ref.dtype).reshape(o_ref.shape)


def _ssm(u, b_slab, lam_re, lam_im, c_slab, d_skip, w_glu, b_glu, *, t_steps):
    batch, s, ssw = u.shape
    n_state = lam_re.shape[1]
    assert batch == SUBLANES, "the scan puts the batch on the sublanes"
    blk = t_steps * batch
    r = np.arange(blk)
    perm = np.zeros((blk, blk), np.float32)
    perm[r, (r % batch) * t_steps + r // batch] = 1.0
    perm = jnp.asarray(perm, BF16)
    tblk = pl.BlockSpec((batch, t_steps, ssw), lambda i: (0, i, 0))
    return pl.pallas_call(
        functools.partial(_ssm_kernel, t_steps=t_steps, n_state=n_state),
        grid=(s // t_steps,),
        in_specs=[
            tblk, _const_spec(perm.shape), _const_spec(perm.shape),
            _const_spec(b_slab.shape), _const_spec(lam_re.shape), _const_spec(lam_im.shape),
            _const_spec(c_slab.shape), _const_spec(d_skip.shape), _const_spec(w_glu.shape),
            _const_spec(b_glu.shape),
        ],
        out_specs=tblk,
        out_shape=jax.ShapeDtypeStruct((batch, s, ssw), BF16),
        scratch_shapes=[pltpu.VMEM((blk, 2 * n_state), F32),
                        pltpu.VMEM((batch, 2 * n_state), F32)],
        compiler_params=_params(1),
        name="ssm",
    )(u, perm, perm.T, b_slab, lam_re, lam_im, c_slab, d_skip, w_glu, b_glu)


def _cross_attention(x, gpre_ref, wq_ref, k_ref, v_ref, wo_ref, gpost_ref, heads):
    d = x.shape[1]
    hd = d // heads
    h = _rms(x, gpre_ref[...]).astype(BF16)
    q = (_dot(h, wq_ref[...]) * (hd ** -0.5)).astype(BF16)
    outs = []
    for i in range(heads):
        sl = slice(i * hd, (i + 1) * hd)
        s = _dot_nt(q[:, sl], k_ref[:, sl])
        p = jnp.exp(s - jnp.max(s, axis=-1, keepdims=True))
        inv = 1.0 / jnp.sum(p, axis=-1, keepdims=True)
        outs.append((_dot(p.astype(BF16), v_ref[:, sl]) * inv).astype(BF16))
    y = _dot(jnp.concatenate(outs, axis=1), wo_ref[...])
    return x + _rms(y, gpost_ref[...])


def _mix_xattn_kernel(oa_ref, os_ref, ga_ref, gs_ref, x_ref, wa_ref, ws_ref, wo_ref, gmix_ref,
                      gpre_ref, wq_ref, k_ref, v_ref, xwo_ref, gpost_ref, out_ref, *, heads):
    pa = _dot(oa_ref[...], wa_ref[...])
    ps = _dot(os_ref[...], ws_ref[...])
    merged = ga_ref[...].astype(F32) * pa + gs_ref[...].astype(F32) * ps
    y = _dot(merged.astype(BF16), wo_ref[...])
    x1 = x_ref[...] + _rms(y, gmix_ref[...])
    out_ref[...] = _cross_attention(x1, gpre_ref, wq_ref, k_ref, v_ref, xwo_ref, gpost_ref, heads)


def _mix_xattn(o_attn, o_ssm, ga, gs, x, wa, ws, wo, gmix, gpre, wq, kx, vx, xwo, gpost, *, ts):
    b, s, d = x.shape
    m = kx.shape[1]
    tok = lambda w: pl.BlockSpec((None, ts, w), lambda bi, ti: (bi, ti, 0))
    memb = pl.BlockSpec((None, m, d), lambda bi, ti: (bi, 0, 0))
    consts = (wa, ws, wo, gmix, gpre, wq)
    return pl.pallas_call(
        functools.partial(_mix_xattn_kernel, heads=XA_HEADS),
        grid=(b, s // ts),
        in_specs=[tok(o_attn.shape[2]), tok(o_ssm.shape[2]), tok(d), tok(d), tok(d)]
        + [_const_spec(a.shape) for a in consts]
        + [memb, memb, _const_spec(xwo.shape), _const_spec(gpost.shape)],
        out_specs=tok(d),
        out_shape=jax.ShapeDtypeStruct((b, s, d), F32),
        compiler_params=_params(2),
        name="mix_xattn",
    )(o_attn, o_ssm, ga, gs, x, *consts, kx, vx, xwo, gpost)


def _memkv_kernel(m_ref, g_ref, wk_ref, wv_ref, k_ref, v_ref):
    mn = _rms(m_ref[...], g_ref[...]).astype(BF16)
    k_ref[...] = _dot(mn, wk_ref[...]).astype(BF16)
    v_ref[...] = _dot(mn, wv_ref[...]).astype(BF16)


def _mem_kv(mem, gain, wk, wv):
    b, m, d = mem.shape
    blk = pl.BlockSpec((None, m, d), lambda bi: (bi, 0, 0))
    out = jax.ShapeDtypeStruct((b, m, d), BF16)
    return pl.pallas_call(
        _memkv_kernel,
        grid=(b,),
        in_specs=[blk, _const_spec(gain.shape), _const_spec(wk.shape), _const_spec(wv.shape)],
        out_specs=(blk, blk),
        out_shape=(out, out),
        compiler_params=_params(1),
        name="mem_kv",
    )(mem, gain, wk, wv)


def _ffn_kernel(x_ref, gpre_ref, wup_ref, cw_ref, cb_ref, wdn_ref, gpost_ref, out_ref,
                carry_ref, acc_ref, h_ref, up_ref, a_ref, *, d_ff, cwid):
    ts = x_ref.shape[0]
    halo = SUBLANES
    n_chunk = d_ff // cwid

    @pl.when(pl.program_id(1) == 0)
    def _():
        carry_ref[...] = jnp.zeros_like(carry_ref)

    h_ref[...] = _rms(x_ref[...], gpre_ref[...]).astype(BF16)

    def stage_up(c, slot):
        for half in range(2):
            c0 = half * d_ff + c * cwid
            up_ref[slot, half] = _dot(h_ref[...], wup_ref[:, c0:c0 + cwid])

    def conv(c0, up):
        ext = jnp.concatenate([carry_ref[:, c0:c0 + cwid], up], axis=0)
        carry_ref[:, c0:c0 + cwid] = up[ts - halo:, :]
        out = cb_ref[:, c0:c0 + cwid] + cw_ref[CONV_WIDTH - 1:CONV_WIDTH, c0:c0 + cwid] * up
        for back in range(1, CONV_WIDTH):
            tap = pltpu.roll(ext, back, 0)[halo:, :]
            out = out + cw_ref[CONV_WIDTH - 1 - back:CONV_WIDTH - back, c0:c0 + cwid] * tap
        return out

    def stage_act(c, slot):
        gate = conv(c * cwid, up_ref[slot, 0])
        val = conv(d_ff + c * cwid, up_ref[slot, 1])
        a_ref[slot] = (_gelu_tanh(gate) * val).astype(BF16)

    def stage_down(c, slot):
        part = _dot(a_ref[slot], wdn_ref[c * cwid:(c + 1) * cwid, :])
        acc_ref[...] = part if c == 0 else acc_ref[...] + part

    for i in range(n_chunk + 2):
        if i >= 2:
            stage_down(i - 2, i % 2)
        if 1 <= i <= n_chunk:
            stage_act(i - 1, (i - 1) % 2)
        if i < n_chunk:
            stage_up(i, i % 2)
    out_ref[...] = x_ref[...] + _rms(acc_ref[...], gpost_ref[...])


def _conv_ffn(x, gpre, w_up, conv_w, conv_b, w_dn, gpost, *, ts, cwid):
    b, s, d = x.shape
    d_ff = w_dn.shape[0]
    tok = pl.BlockSpec((None, ts, d), lambda bi, ti: (bi, ti, 0))
    return pl.pallas_call(
        functools.partial(_ffn_kernel, d_ff=d_ff, cwid=cwid),
        grid=(b, s // ts),
        in_specs=[tok, _const_spec(gpre.shape), _const_spec(w_up.shape), _const_spec(conv_w.shape),
                  _const_spec(conv_b.shape), _const_spec(w_dn.shape), _const_spec(gpost.shape)],
        out_specs=tok,
        out_shape=jax.ShapeDtypeStruct((b, s, d), F32),
        scratch_shapes=[pltpu.VMEM((SUBLANES, 2 * d_ff), F32),
                        pltpu.VMEM((ts, d), F32),
                        pltpu.VMEM((ts, d), BF16),
                        pltpu.VMEM((2, 2, ts, cwid), F32),
                        pltpu.VMEM((2, ts, cwid), BF16)],
        compiler_params=_params(2),
        name="conv_ffn",
    )(x, gpre, w_up, conv_w, conv_b, w_dn, gpost)


def _ssm_slabs(bb_re, bb_im, c_re, c_im):
    g, p, c = bb_re.shape
    gps = LANES // c
    n_slab = g // gps
    eye = jnp.eye(gps, dtype=F32)

    def b_part(bb):
        t = bb.reshape(n_slab, gps, p, c)
        return jnp.einsum('sgpc,gh->sgchp', t, eye).reshape(n_slab, gps * c, gps * p)

    def c_part(cc):
        t = cc.reshape(n_slab, gps, c, p)
        return jnp.einsum('sgcp,gh->shpgc', t, eye).reshape(n_slab, gps * p, gps * c)

    b_slab = jnp.concatenate([b_part(bb_re), b_part(bb_im)], axis=2).astype(BF16)
    c_slab = jnp.concatenate([c_part(c_re), c_part(-c_im)], axis=1).astype(BF16)
    return b_slab, c_slab


def _layer(x, mem, l, p):
    b, s, d = x.shape
    sbw = p["w_branch_attn"].shape[1]
    ssw = p["w_branch_ssm"].shape[1]
    row = lambda a: a[l][None, :].astype(F32)
    wb = lambda a: a[l].astype(BF16)

    lam_re, lam_im, bb_re, bb_im = _discretize(
        p["ssm_a_re"][l], p["ssm_a_im"][l], p["ssm_log_dt"][l], p["ssm_b_re"][l], p["ssm_b_im"][l])
    b_slab, c_slab = _ssm_slabs(bb_re, bb_im, p["ssm_c_re"][l].astype(F32), p["ssm_c_im"][l].astype(F32))
    n_state = lam_re.size
    lam_re_b = jnp.broadcast_to(lam_re.reshape(1, n_state), (SUBLANES, n_state))
    lam_im_b = jnp.broadcast_to(lam_im.reshape(1, n_state), (SUBLANES, n_state))

    qt, k, vt, u, ga, gs = _in_proj(x, row(p["norm_mix_pre"]), wb(p["w_in"]), row(p["b_gate"]),
                                       sbw=sbw, ssw=ssw, ts=512, tq=256)
    o_attn = _sb_attn(qt, k, vt, tq=256)
    o_ssm = _ssm(u, b_slab, lam_re_b, lam_im_b, c_slab,
                 row(p["ssm_d"]), wb(p["ssm_w_glu"]), row(p["ssm_b_glu"]), t_steps=64)
    kx, vx = _mem_kv(mem, row(p["norm_mem"]), wb(p["xa_wk"]), wb(p["xa_wv"]))
    x = _mix_xattn(o_attn, o_ssm, ga, gs, x, wb(p["w_branch_attn"]), wb(p["w_branch_ssm"]),
                   wb(p["w_out"]), row(p["norm_mix_post"]), row(p["norm_xa_pre"]), wb(p["xa_wq"]),
                   kx, vx, wb(p["xa_wo"]), row(p["norm_xa_post"]), ts=512)

    x = _conv_ffn(x, row(p["norm_ffn_pre"]), wb(p["ffn_w_up"]), p["ffn_conv_w"][l].astype(F32),
                  row(p["ffn_conv_b"]), wb(p["ffn_w_down"]), row(p["norm_ffn_post"]), ts=512, cwid=256)
    return x


def kernel(x, mem, norm_mix_pre, norm_mix_post, w_in, b_gate, ssm_a_re, ssm_a_im, ssm_log_dt, ssm_b_re, ssm_b_im, ssm_c_re, ssm_c_im, ssm_d, ssm_w_glu, ssm_b_glu, w_branch_attn, w_branch_ssm, w_out, norm_xa_pre, norm_xa_post, norm_mem, xa_wq, xa_wk, xa_wv, xa_wo, norm_ffn_pre, norm_ffn_post, ffn_w_up, ffn_conv_w, ffn_conv_b, ffn_w_down):
    p = dict(norm_mix_pre=norm_mix_pre, norm_mix_post=norm_mix_post, w_in=w_in, b_gate=b_gate,
             ssm_a_re=ssm_a_re, ssm_a_im=ssm_a_im, ssm_log_dt=ssm_log_dt, ssm_b_re=ssm_b_re,
             ssm_b_im=ssm_b_im, ssm_c_re=ssm_c_re, ssm_c_im=ssm_c_im, ssm_d=ssm_d,
             ssm_w_glu=ssm_w_glu, ssm_b_glu=ssm_b_glu, w_branch_attn=w_branch_attn,
             w_branch_ssm=w_branch_ssm, w_out=w_out, norm_xa_pre=norm_xa_pre,
             norm_xa_post=norm_xa_post, norm_mem=norm_mem, xa_wq=xa_wq, xa_wk=xa_wk, xa_wv=xa_wv,
             xa_wo=xa_wo, norm_ffn_pre=norm_ffn_pre, norm_ffn_post=norm_ffn_post,
             ffn_w_up=ffn_w_up, ffn_conv_w=ffn_conv_w, ffn_conv_b=ffn_conv_b, ffn_w_down=ffn_w_down)
    for l in range(w_in.shape[0]):
        x = _layer(x, mem, l, p)
    return x
```

```python
import functools
import math

import numpy as np

import jax
import jax.numpy as jnp
from jax import lax
from jax.experimental import pallas as pl
from jax.experimental.pallas import tpu as pltpu

F32 = jnp.float32
BF16 = jnp.bfloat16

RMS_EPS = 1e-6
SB_HEAD_DIM = 64
SSM_GROUP = 16
SSM_STATE = 64
XA_HEADS = 4
CONV_WIDTH = 3
LANES = 128
SUBLANES = 8
VMEM_LIMIT = 56 * 1024 * 1024
LOG2E = 1.4426950408889634


def _rms(x, g):
    ms = jnp.mean(x * x, axis=-1, keepdims=True)
    return x * lax.rsqrt(ms + RMS_EPS) * g


def _dot(a, b):
    return jnp.dot(a, b, preferred_element_type=F32)


def _gelu_tanh(x):
    a = -2.0 * math.sqrt(2.0 / math.pi) * LOG2E
    return x / (1.0 + jnp.exp2(x * (a + (a * 0.044715) * (x * x))))


def _dot_nt(a, b):
    return lax.dot_general(a, b, (((1,), (1,)), ((), ())), preferred_element_type=F32)


def _params(n_axes):
    return pltpu.CompilerParams(
        dimension_semantics=("arbitrary",) * n_axes, vmem_limit_bytes=VMEM_LIMIT)


def _const_spec(shape):
    zeros = (0,) * len(shape)
    return pl.BlockSpec(shape, lambda *_: zeros, pipeline_mode=pl.Buffered(1))


def _discretize_kernel(are_ref, aim_ref, ldt_ref, bre_ref, bim_ref,
                       lre_ref, lim_ref, bbre_ref, bbim_ref):
    a_re = are_ref[...]
    a_im = aim_ref[...]
    dt = jnp.exp(ldt_ref[...])
    mag = jnp.exp(a_re * dt)
    l_re = mag * jnp.cos(a_im * dt)
    l_im = mag * jnp.sin(a_im * dt)
    lre_ref[...] = l_re
    lim_ref[...] = l_im
    n_re = l_re - 1.0
    inv = 1.0 / (a_re * a_re + a_im * a_im)
    c_re = (n_re * a_re + l_im * a_im) * inv
    c_im = (l_im * a_re - n_re * a_im) * inv
    b_re = bre_ref[...]
    b_im = bim_ref[...]
    bbre_ref[...] = c_re * b_re - c_im * b_im
    bbim_ref[...] = c_re * b_im + c_im * b_re


def _discretize(a_re, a_im, log_dt, b_re, b_im):
    g, p, c = b_re.shape
    rep = lambda a: jnp.repeat(a, c, axis=-1)
    args = (rep(a_re), rep(a_im), jnp.broadcast_to(log_dt[:, None], (g, p * c)),
            b_re.reshape(g, p * c), b_im.reshape(g, p * c))
    out = jax.ShapeDtypeStruct((g, p * c), F32)
    l_re, l_im, bb_re, bb_im = pl.pallas_call(
        _discretize_kernel, out_shape=(out, out, out, out), name="ssm_discretize")(*args)
    return (l_re[:, ::c], l_im[:, ::c], bb_re.reshape(g, p, c), bb_im.reshape(g, p, c))


def _inproj_kernel(x_ref, g_ref, w_ref, bg_ref, perm_ref, qt_ref, k_ref, vt_ref, u_ref, ga_ref, gs_ref,
                   *, sbw, ssw, d, qscale, tq):
    ts = x_ref.shape[0]
    h = _rms(x_ref[...], g_ref[...]).astype(BF16)

    def proj(c0, n):
        return _dot(h, w_ref[:, c0:c0 + n])

    q = proj(0, sbw) * qscale
    k = proj(sbw, sbw).astype(BF16)
    v = proj(2 * sbw, sbw).astype(BF16)
    for c in range(ts // tq):
        rows = slice(c * tq, (c + 1) * tq)
        k_ref[rows, :] = _dot(perm_ref[...], k[rows, :]).astype(BF16)
        vp = _dot(perm_ref[...], v[rows, :])
        for hp in range(sbw // LANES):
            cols = slice(hp * LANES, (hp + 1) * LANES)
            qt_ref[hp, c] = q[rows, cols].T.astype(BF16)
            vt_ref[hp, c] = vp[:, cols].T.astype(BF16)
    u_ref[...] = proj(3 * sbw, ssw).astype(BF16)
    g0 = 3 * sbw + ssw
    cw = 512
    for c in range(d // cw):
        sl = slice(c * cw, (c + 1) * cw)
        ga_ref[:, sl] = jax.nn.sigmoid(proj(g0 + c * cw, cw) + bg_ref[:, sl]).astype(BF16)
        gs_ref[:, sl] = jax.nn.sigmoid(
            proj(g0 + d + c * cw, cw) + bg_ref[:, d + c * cw:d + (c + 1) * cw]).astype(BF16)


def _key_order(tk):
    r = np.arange(tk)
    return (r % SUBLANES) * (tk // SUBLANES) + r // SUBLANES


def _in_proj(x, gain, w_in, b_gate, *, sbw, ssw, ts, tq):
    b, s, d = x.shape
    in_w = w_in.shape[1]
    hp = sbw // LANES
    qscale = SB_HEAD_DIM ** -0.5 * LOG2E
    tok = lambda w: pl.BlockSpec((None, ts, w), lambda bi, ti: (bi, ti, 0))
    tr = pl.BlockSpec((None, hp, ts // tq, LANES, tq), lambda bi, ti: (bi, 0, ti, 0, 0))
    out_shape = (
        jax.ShapeDtypeStruct((b, hp, s // tq, LANES, tq), BF16),
        jax.ShapeDtypeStruct((b, s, sbw), BF16),
        jax.ShapeDtypeStruct((b, hp, s // tq, LANES, tq), BF16),
        jax.ShapeDtypeStruct((b, s, ssw), BF16),
        jax.ShapeDtypeStruct((b, s, d), BF16),
        jax.ShapeDtypeStruct((b, s, d), BF16),
    )
    perm = np.zeros((tq, tq), np.float32)
    perm[np.arange(tq), _key_order(tq)] = 1.0
    perm = jnp.asarray(perm, BF16)
    return pl.pallas_call(
        functools.partial(_inproj_kernel, sbw=sbw, ssw=ssw, d=d, qscale=qscale, tq=tq),
        grid=(b, s // ts),
        in_specs=[tok(d), _const_spec((1, d)), _const_spec((d, in_w)), _const_spec((1, 2 * d)),
                  _const_spec(perm.shape)],
        out_specs=(tr, tok(sbw), tr, tok(ssw), tok(d), tok(d)),
        out_shape=out_shape,
        compiler_params=_params(2),
        name="in_proj",
    )(x, gain, w_in, b_gate, perm)


_SB_DEPTH = 3
_SB_SLOTS = 2
_SB_UNROLL = 4
_SB_MASK_BIAS = -1e30


def _sb_item_table(nq):
    phases = [[(qi, qi) for qi in range(nq)],
              [(qi, qi - j) for qi in range(nq) for j in range(1, qi + 1)]]
    cols, bounds = [], [0]
    for items in phases:
        n = len(items)
        n_steps = -(-(n + _SB_DEPTH - 1) // _SB_UNROLL) * _SB_UNROLL if n else 0
        for t in range(n_steps):
            a = items[min(t, n - 1)]
            c = items[t - 1][0] if 0 <= t - 1 < n else nq
            d = items[t - 2] if 0 <= t - 2 < n else (nq, 0)
            cols.append((a[0], a[1], c, d[0], d[1]))
        bounds.append(len(cols))
    return np.asarray(cols, np.int32).T.copy(), bounds


def _sb_attn_kernel(tab_ref, qt_ref, k_ref, vt_ref, bias_ref, o_ref,
                    z_ref, w_ref, acc_ref, r_ref, *, tq, bounds):
    nq = qt_ref.shape[0]
    n_tiles = tq // SUBLANES
    drow = lax.broadcasted_iota(jnp.int32, (LANES, tq), 0)
    head_rows = [(drow // SB_HEAD_DIM) == h for h in range(2)]
    sub = lax.broadcasted_iota(jnp.int32, (SUBLANES, tq), 0)

    for ref in (z_ref, w_ref, acc_ref, r_ref):
        ref[...] = jnp.zeros_like(ref)

    def m1(t, s, diagonal):
        qt = qt_ref[tab_ref[0, t]]
        ks = pl.multiple_of(tab_ref[1, t] * tq, tq)
        kc = k_ref[pl.ds(ks, tq), :]
        for h in range(2):
            z = _dot(kc, jnp.where(head_rows[h], qt, jnp.zeros_like(qt)))
            z_ref[s, h] = z + bias_ref[...] if diagonal else z

    def v(t, s, diagonal):
        qi = tab_ref[2, t]
        for h in range(2):
            f = 1.0 / (1.0 + jnp.exp2(z_ref[s, h]))
            p = [None] * n_tiles
            p[-1] = f[(n_tiles - 1) * SUBLANES:, :]
            for i in range(n_tiles - 2, -1, -1):
                p[i] = f[i * SUBLANES:(i + 1) * SUBLANES, :] * p[i + 1]
            x = p[0]
            for step in (1, 2, 4):
                x = x * jnp.where(sub + step < SUBLANES, pltpu.roll(x, SUBLANES - step, 0), 1.0)
            g = jnp.where(sub + 1 < SUBLANES, pltpu.roll(x, SUBLANES - 1, 0), 1.0)
            total = jnp.broadcast_to(x[0:1, :], x.shape)
            if diagonal:
                r_ref[qi, h] = total
            else:
                r = r_ref[qi, h]
                g = g * r
                r_ref[qi, h] = r * total
            e = [p[i] * g for i in range(n_tiles)] + [g]
            w = jnp.concatenate([e[i + 1] - e[i] for i in range(n_tiles)], axis=0)
            w_ref[s, h] = w.astype(BF16)

    def m3(t, s, diagonal):
        qi = tab_ref[3, t]
        vc = vt_ref[tab_ref[4, t]]
        for h in range(2):
            pv = _dot(vc[h * SB_HEAD_DIM:(h + 1) * SB_HEAD_DIM, :], w_ref[s, h])
            acc_ref[qi, h] = pv if diagonal else acc_ref[qi, h] + pv

    def run_phase(first_step, last_step, diagonal):
        def outer(i, carry):
            for u in range(_SB_UNROLL):
                t = i * _SB_UNROLL + u
                cur, prev = u % _SB_SLOTS, (u - 1) % _SB_SLOTS
                m3(t, cur, diagonal)
                v(t, prev, diagonal)
                m1(t, cur, diagonal)
            return carry

        lax.fori_loop(first_step // _SB_UNROLL, last_step // _SB_UNROLL, outer, 0)

    run_phase(bounds[0], bounds[1], True)
    run_phase(bounds[1], bounds[2], False)
    for qi in range(nq):
        acc = jnp.concatenate([acc_ref[qi, 0], acc_ref[qi, 1]], axis=0)
        o_ref[qi * tq:(qi + 1) * tq, :] = acc.T.astype(o_ref.dtype)


def _sb_attn(qt, k, vt, *, tq):
    b, hp, nq, _, _ = qt.shape
    s = k.shape[1]
    tab, bounds = _sb_item_table(nq)
    key = _key_order(tq)[:, None]
    bias = jnp.asarray(np.where(key < np.arange(tq)[None, :], 0.0, _SB_MASK_BIAS), F32)
    tr = pl.BlockSpec((None, None, nq, LANES, tq), lambda bi, hi, tab: (bi, hi, 0, 0, 0))
    tokb = pl.BlockSpec((None, s, LANES), lambda bi, hi, tab: (bi, 0, hi))
    grid_spec = pltpu.PrefetchScalarGridSpec(
        num_scalar_prefetch=1,
        grid=(b, hp),
        in_specs=[tr, tokb, tr,
                  pl.BlockSpec(bias.shape, lambda bi, hi, tab: (0, 0), pipeline_mode=pl.Buffered(1))],
        out_specs=tokb,
        scratch_shapes=[
            pltpu.VMEM((_SB_SLOTS, 2, tq, tq), F32),
            pltpu.VMEM((_SB_SLOTS, 2, tq, tq), BF16),
            pltpu.VMEM((nq + 1, 2, SB_HEAD_DIM, tq), F32),
            pltpu.VMEM((nq + 1, 2, SUBLANES, tq), F32),
        ],
    )
    return pl.pallas_call(
        functools.partial(_sb_attn_kernel, tq=tq, bounds=tuple(bounds)),
        grid_spec=grid_spec,
        out_shape=jax.ShapeDtypeStruct((b, s, hp * LANES), BF16),
        compiler_params=_params(2),
        name="sb_attn",
    )(jnp.asarray(tab), qt, k, vt, bias)


def _ssm_kernel(u_ref, perm_ref, permt_ref, bs_ref, are_ref, aim_ref, cs_ref, d_ref, wg_ref, bg_ref,
                o_ref, bu_ref, xs_ref, *, t_steps, n_state):
    nb = u_ref.shape[0]
    n_slab = u_ref.shape[2] // LANES
    sw = n_state // n_slab

    @pl.when(pl.program_id(0) == 0)
    def _():
        xs_ref[...] = jnp.zeros_like(xs_ref)

    u_bt = u_ref[...].reshape(nb * t_steps, u_ref.shape[2])
    u = _dot(perm_ref[...], u_bt)
    ub = u.astype(BF16)
    for s in range(n_slab):
        bu = _dot(ub[:, s * LANES:(s + 1) * LANES], bs_ref[s])
        bu_ref[:, s * sw:(s + 1) * sw] = bu[:, :sw]
        bu_ref[:, n_state + s * sw:n_state + (s + 1) * sw] = bu[:, sw:]

    n_part = 2
    pw = n_state // n_part
    for part in range(n_part):
        re = slice(part * pw, (part + 1) * pw)
        im = slice(n_state + part * pw, n_state + (part + 1) * pw)
        a_re = are_ref[:, re]
        a_im = aim_ref[:, re]

        def body(t, carry, re=re, im=im, a_re=a_re, a_im=a_im):
            x_re, x_im = carry
            rows = pl.ds(pl.multiple_of(t * nb, nb), nb)
            n_re = a_re * x_re - a_im * x_im + bu_ref[rows, re]
            n_im = a_re * x_im + a_im * x_re + bu_ref[rows, im]
            bu_ref[rows, re] = n_re
            bu_ref[rows, im] = n_im
            return n_re, n_im

        x_re, x_im = lax.fori_loop(0, t_steps, body, (xs_ref[:, re], xs_ref[:, im]), unroll=2)
        xs_ref[:, re] = x_re
        xs_ref[:, im] = x_im

    ys = []
    for s in range(n_slab):
        x_re = bu_ref[:, s * sw:(s + 1) * sw].astype(BF16)
        x_im = bu_ref[:, n_state + s * sw:n_state + (s + 1) * sw].astype(BF16)
        ys.append(_dot(x_re, cs_ref[s, :sw, :]) + _dot(x_im, cs_ref[s, sw:, :]))
    y = jnp.concatenate(ys, axis=1) + d_ref[...] * u
    y = _gelu_tanh(y)
    gate = jax.nn.sigmoid(_dot(y.astype(BF16), wg_ref[...]) + bg_ref[...])
    out = _dot(permt_ref[...], (y * gate).astype(BF16))
    o_ref[...] = out.astype(o_ref.dtype).reshape(o_ref.shape)


def _ssm(u, b_slab, lam_re, lam_im, c_slab, d_skip, w_glu, b_glu, *, t_steps):
    batch, s, ssw = u.shape
    n_state = lam_re.shape[1]
    assert batch == SUBLANES, "the scan puts the batch on the sublanes"
    blk = t_steps * batch
    r = np.arange(blk)
    perm = np.zeros((blk, blk), np.float32)
    perm[r, (r % batch) * t_steps + r // batch] = 1.0
    perm = jnp.asarray(perm, BF16)
    tblk = pl.BlockSpec((batch, t_steps, ssw), lambda i: (0, i, 0))
    return pl.pallas_call(
        functools.partial(_ssm_kernel, t_steps=t_steps, n_state=n_state),
        grid=(s // t_steps,),
        in_specs=[
            tblk, _const_spec(perm.shape), _const_spec(perm.shape),
            _const_spec(b_slab.shape), _const_spec(lam_re.shape), _const_spec(lam_im.shape),
            _const_spec(c_slab.shape), _const_spec(d_skip.shape), _const_spec(w_glu.shape),
            _const_spec(b_glu.shape),
        ],
        out_specs=tblk,
        out_shape=jax.ShapeDtypeStruct((batch, s, ssw), BF16),
        scratch_shapes=[pltpu.VMEM((blk, 2 * n_state), F32),
                        pltpu.VMEM((batch, 2 * n_state), F32)],
        compiler_params=_params(1),
        name="ssm",
    )(u, perm, perm.T, b_slab, lam_re, lam_im, c_slab, d_skip, w_glu, b_glu)


def _cross_attention(x, gpre_ref, wq_ref, k_ref, v_ref, wo_ref, gpost_ref, heads):
    d = x.shape[1]
    hd = d // heads
    h = _rms(x, gpre_ref[...]).astype(BF16)
    q = (_dot(h, wq_ref[...]) * (hd ** -0.5)).astype(BF16)
    outs = []
    for i in range(heads):
        sl = slice(i * hd, (i + 1) * hd)
        s = _dot_nt(q[:, sl], k_ref[:, sl])
        p = jnp.exp(s - jnp.max(s, axis=-1, keepdims=True))
        inv = 1.0 / jnp.sum(p, axis=-1, keepdims=True)
        outs.append((_dot(p.astype(BF16), v_ref[:, sl]) * inv).astype(BF16))
    y = _dot(jnp.concatenate(outs, axis=1), wo_ref[...])
    return x + _rms(y, gpost_ref[...])


def _mix_xattn_kernel(oa_ref, os_ref, ga_ref, gs_ref, x_ref, wa_ref, ws_ref, wo_ref, gmix_ref,
                      gpre_ref, wq_ref, k_ref, v_ref, xwo_ref, gpost_ref, out_ref, *, heads):
    pa = _dot(oa_ref[...], wa_ref[...])
    ps = _dot(os_ref[...], ws_ref[...])
    merged = ga_ref[...].astype(F32) * pa + gs_ref[...].astype(F32) * ps
    y = _dot(merged.astype(BF16), wo_ref[...])
    x1 = x_ref[...] + _rms(y, gmix_ref[...])
    out_ref[...] = _cross_attention(x1, gpre_ref, wq_ref, k_ref, v_ref, xwo_ref, gpost_ref, heads)


def _mix_xattn(o_attn, o_ssm, ga, gs, x, wa, ws, wo, gmix, gpre, wq, kx, vx, xwo, gpost, *, ts):
    b, s, d = x.shape
    m = kx.shape[1]
    tok = lambda w: pl.BlockSpec((None, ts, w), lambda bi, ti: (bi, ti, 0))
    memb = pl.BlockSpec((None, m, d), lambda bi, ti: (bi, 0, 0))
    consts = (wa, ws, wo, gmix, gpre, wq)
    return pl.pallas_call(
        functools.partial(_mix_xattn_kernel, heads=XA_HEADS),
        grid=(b, s // ts),
        in_specs=[tok(o_attn.shape[2]), tok(o_ssm.shape[2]), tok(d), tok(d), tok(d)]
        + [_const_spec(a.shape) for a in consts]
        + [memb, memb, _const_spec(xwo.shape), _const_spec(gpost.shape)],
        out_specs=tok(d),
        out_shape=jax.ShapeDtypeStruct((b, s, d), F32),
        compiler_params=_params(2),
        name="mix_xattn",
    )(o_attn, o_ssm, ga, gs, x, *consts, kx, vx, xwo, gpost)


def _memkv_kernel(m_ref, g_ref, wk_ref, wv_ref, k_ref, v_ref):
    mn = _rms(m_ref[...], g_ref[...]).astype(BF16)
    k_ref[...] = _dot(mn, wk_ref[...]).astype(BF16)
    v_ref[...] = _dot(mn, wv_ref[...]).astype(BF16)


def _mem_kv(mem, gain, wk, wv):
    b, m, d = mem.shape
    blk = pl.BlockSpec((None, m, d), lambda bi: (bi, 0, 0))
    out = jax.ShapeDtypeStruct((b, m, d), BF16)
    return pl.pallas_call(
        _memkv_kernel,
        grid=(b,),
        in_specs=[blk, _const_spec(gain.shape), _const_spec(wk.shape), _const_spec(wv.shape)],
        out_specs=(blk, blk),
        out_shape=(out, out),
        compiler_params=_params(1),
        name="mem_kv",
    )(mem, gain, wk, wv)


def _ffn_kernel(x_ref, gpre_ref, wup_ref, cw_ref, cb_ref, wdn_ref, gpost_ref, out_ref,
                carry_ref, acc_ref, h_ref, up_ref, a_ref, *, d_ff, cwid):
    ts = x_ref.shape[0]
    halo = SUBLANES
    n_chunk = d_ff // cwid

    @pl.when(pl.program_id(1) == 0)
    def _():
        carry_ref[...] = jnp.zeros_like(carry_ref)

    h_ref[...] = _rms(x_ref[...], gpre_ref[...]).astype(BF16)

    def stage_up(c, slot):
        for half in range(2):
            c0 = half * d_ff + c * cwid
            up_ref[slot, half] = _dot(h_ref[...], wup_ref[:, c0:c0 + cwid])

    def conv(c0, up):
        ext = jnp.concatenate([carry_ref[:, c0:c0 + cwid], up], axis=0)
        carry_ref[:, c0:c0 + cwid] = up[ts - halo:, :]
        out = cb_ref[:, c0:c0 + cwid] + cw_ref[CONV_WIDTH - 1:CONV_WIDTH, c0:c0 + cwid] * up
        for back in range(1, CONV_WIDTH):
            tap = pltpu.roll(ext, back, 0)[halo:, :]
            out = out + cw_ref[CONV_WIDTH - 1 - back:CONV_WIDTH - back, c0:c0 + cwid] * tap
        return out

    def stage_act(c, slot):
        gate = conv(c * cwid, up_ref[slot, 0])
        val = conv(d_ff + c * cwid, up_ref[slot, 1])
        a_ref[slot] = (_gelu_tanh(gate) * val).astype(BF16)

    def stage_down(c, slot):
        part = _dot(a_ref[slot], wdn_ref[c * cwid:(c + 1) * cwid, :])
        acc_ref[...] = part if c == 0 else acc_ref[...] + part

    for i in range(n_chunk + 2):
        if i >= 2:
            stage_down(i - 2, i % 2)
        if 1 <= i <= n_chunk:
            stage_act(i - 1, (i - 1) % 2)
        if i < n_chunk:
            stage_up(i, i % 2)
    out_ref[...] = x_ref[...] + _rms(acc_ref[...], gpost_ref[...])


def _conv_ffn(x, gpre, w_up, conv_w, conv_b, w_dn, gpost, *, ts, cwid):
    b, s, d = x.shape
    d_ff = w_dn.shape[0]
    tok = pl.BlockSpec((None, ts, d), lambda bi, ti: (bi, ti, 0))
    return pl.pallas_call(
        functools.partial(_ffn_kernel, d_ff=d_ff, cwid=cwid),
        grid=(b, s // ts),
        in_specs=[tok, _const_spec(gpre.shape), _const_spec(w_up.shape), _const_spec(conv_w.shape),
                  _const_spec(conv_b.shape), _const_spec(w_dn.shape), _const_spec(gpost.shape)],
        out_specs=tok,
        out_shape=jax.ShapeDtypeStruct((b, s, d), F32),
        scratch_shapes=[pltpu.VMEM((SUBLANES, 2 * d_ff), F32),
                        pltpu.VMEM((ts, d), F32),
                        pltpu.VMEM((ts, d), BF16),
                        pltpu.VMEM((2, 2, ts, cwid), F32),
                        pltpu.VMEM((2, ts, cwid), BF16)],
        compiler_params=_params(2),
        name="conv_ffn",
    )(x, gpre, w_up, conv_w, conv_b, w_dn, gpost)


def _ssm_slabs(bb_re, bb_im, c_re, c_im):
    g, p, c = bb_re.shape
    gps = LANES // c
    n_slab = g // gps
    eye = jnp.eye(gps, dtype=F32)

    def b_part(bb):
        t = bb.reshape(n_slab, gps, p, c)
        return jnp.einsum('sgpc,gh->sgchp', t, eye).reshape(n_slab, gps * c, gps * p)

    def c_part(cc):
        t = cc.reshape(n_slab, gps, c, p)
        return jnp.einsum('sgcp,gh->shpgc', t, eye).reshape(n_slab, gps * p, gps * c)

    b_slab = jnp.concatenate([b_part(bb_re), b_part(bb_im)], axis=2).astype(BF16)
    c_slab = jnp.concatenate([c_part(c_re), c_part(-c_im)], axis=1).astype(BF16)
    return b_slab, c_slab


def _layer(x, mem, l, p):
    b, s, d = x.shape
    sbw = p["w_branch_attn"].shape[1]
    ssw = p["w_branch_ssm"].shape[1]
    row = lambda a: a[l][None, :].astype(F32)
    wb = lambda a: a[l].astype(BF16)

    lam_re, lam_im, bb_re, bb_im = _discretize(
        p["ssm_a_re"][l], p["ssm_a_im"][l], p["ssm_log_dt"][l], p["ssm_b_re"][l], p["ssm_b_im"][l])
    b_slab, c_slab = _ssm_slabs(bb_re, bb_im, p["ssm_c_re"][l].astype(F32), p["ssm_c_im"][l].astype(F32))
    n_state = lam_re.size
    lam_re_b = jnp.broadcast_to(lam_re.reshape(1, n_state), (SUBLANES, n_state))
    lam_im_b = jnp.broadcast_to(lam_im.reshape(1, n_state), (SUBLANES, n_state))

    qt, k, vt, u, ga, gs = _in_proj(x, row(p["norm_mix_pre"]), wb(p["w_in"]), row(p["b_gate"]),
                                       sbw=sbw, ssw=ssw, ts=512, tq=256)
    o_attn = _sb_attn(qt, k, vt, tq=256)
    o_ssm = _ssm(u, b_slab, lam_re_b, lam_im_b, c_slab,
                 row(p["ssm_d"]), wb(p["ssm_w_glu"]), row(p["ssm_b_glu"]), t_steps=64)
    kx, vx = _mem_kv(mem, row(p["norm_mem"]), wb(p["xa_wk"]), wb(p["xa_wv"]))
    x = _mix_xattn(o_attn, o_ssm, ga, gs, x, wb(p["w_branch_attn"]), wb(p["w_branch_ssm"]),
                   wb(p["w_out"]), row(p["norm_mix_post"]), row(p["norm_xa_pre"]), wb(p["xa_wq"]),
                   kx, vx, wb(p["xa_wo"]), row(p["norm_xa_post"]), ts=512)

    x = _conv_ffn(x, row(p["norm_ffn_pre"]), wb(p["ffn_w_up"]), p["ffn_conv_w"][l].astype(F32),
                  row(p["ffn_conv_b"]), wb(p["ffn_w_down"]), row(p["norm_ffn_post"]), ts=512, cwid=256)
    return x


def kernel(x, mem, norm_mix_pre, norm_mix_post, w_in, b_gate, ssm_a_re, ssm_a_im, ssm_log_dt, ssm_b_re, ssm_b_im, ssm_c_re, ssm_c_im, ssm_d, ssm_w_glu, ssm_b_glu, w_branch_attn, w_branch_ssm, w_out, norm_xa_pre, norm_xa_post, norm_mem, xa_wq, xa_wk, xa_wv, xa_wo, norm_ffn_pre, norm_ffn_post, ffn_w_up, ffn_conv_w, ffn_conv_b, ffn_w_down):
    p = dict(norm_mix_pre=norm_mix_pre, norm_mix_post=norm_mix_post, w_in=w_in, b_gate=b_gate,
             ssm_a_re=ssm_a_re, ssm_a_im=ssm_a_im, ssm_log_dt=ssm_log_dt, ssm_b_re=ssm_b_re,
             ssm_b_im=ssm_b_im, ssm_c_re=ssm_c_re, ssm_c_im=ssm_c_im, ssm_d=ssm_d,
             ssm_w_glu=ssm_w_glu, ssm_b_glu=ssm_b_glu, w_branch_attn=w_branch_attn,
             w_branch_ssm=w_branch_ssm, w_out=w_out, norm_xa_pre=norm_xa_pre,
             norm_xa_post=norm_xa_post, norm_mem=norm_mem, xa_wq=xa_wq, xa_wk=xa_wk, xa_wv=xa_wv,
             xa_wo=xa_wo, norm_ffn_pre=norm_ffn_pre, norm_ffn_post=norm_ffn_post,
             ffn_w_up=ffn_w_up, ffn_conv_w=ffn_conv_w, ffn_conv_b=ffn_conv_b, ffn_w_down=ffn_w_down)
    for l in range(w_in.shape[0]):
        x = _layer(x, mem, l, p)
    return x
```

```python
import functools
import math

import numpy as np

import jax
import jax.numpy as jnp
from jax import lax
from jax.experimental import pallas as pl
from jax.experimental.pallas import tpu as pltpu

F32 = jnp.float32
BF16 = jnp.bfloat16

RMS_EPS = 1e-6
SB_HEAD_DIM = 64
SSM_GROUP = 16
SSM_STATE = 64
XA_HEADS = 4
CONV_WIDTH = 3
LANES = 128
SUBLANES = 8
VMEM_LIMIT = 56 * 1024 * 1024
LOG2E = 1.4426950408889634


def _rms(x, g):
    ms = jnp.mean(x * x, axis=-1, keepdims=True)
    return x * lax.rsqrt(ms + RMS_EPS) * g


def _dot(a, b):
    return jnp.dot(a, b, preferred_element_type=F32)


def _gelu_tanh(x):
    a = -2.0 * math.sqrt(2.0 / math.pi) * LOG2E
    return x / (1.0 + jnp.exp2(x * (a + (a * 0.044715) * (x * x))))


def _dot_nt(a, b):
    return lax.dot_general(a, b, (((1,), (1,)), ((), ())), preferred_element_type=F32)


def _params(n_axes):
    return pltpu.CompilerParams(
        dimension_semantics=("arbitrary",) * n_axes, vmem_limit_bytes=VMEM_LIMIT)


def _const_spec(shape):
    zeros = (0,) * len(shape)
    return pl.BlockSpec(shape, lambda *_: zeros, pipeline_mode=pl.Buffered(1))


def _discretize_kernel(are_ref, aim_ref, ldt_ref, bre_ref, bim_ref,
                       lre_ref, lim_ref, bbre_ref, bbim_ref):
    a_re = are_ref[...]
    a_im = aim_ref[...]
    dt = jnp.exp(ldt_ref[...])
    mag = jnp.exp(a_re * dt)
    l_re = mag * jnp.cos(a_im * dt)
    l_im = mag * jnp.sin(a_im * dt)
    lre_ref[...] = l_re
    lim_ref[...] = l_im
    n_re = l_re - 1.0
    inv = 1.0 / (a_re * a_re + a_im * a_im)
    c_re = (n_re * a_re + l_im * a_im) * inv
    c_im = (l_im * a_re - n_re * a_im) * inv
    b_re = bre_ref[...]
    b_im = bim_ref[...]
    bbre_ref[...] = c_re * b_re - c_im * b_im
    bbim_ref[...] = c_re * b_im + c_im * b_re


def _discretize(a_re, a_im, log_dt, b_re, b_im):
    g, p, c = b_re.shape
    rep = lambda a: jnp.repeat(a, c, axis=-1)
    args = (rep(a_re), rep(a_im), jnp.broadcast_to(log_dt[:, None], (g, p * c)),
            b_re.reshape(g, p * c), b_im.reshape(g, p * c))
    out = jax.ShapeDtypeStruct((g, p * c), F32)
    l_re, l_im, bb_re, bb_im = pl.pallas_call(
        _discretize_kernel, out_shape=(out, out, out, out), name="ssm_discretize")(*args)
    return (l_re[:, ::c], l_im[:, ::c], bb_re.reshape(g, p, c), bb_im.reshape(g, p, c))


def _inproj_kernel(x_ref, g_ref, w_ref, bg_ref, perm_ref, qt_ref, k_ref, vt_ref, u_ref, ga_ref, gs_ref,
                   *, sbw, ssw, d, qscale, tq):
    ts = x_ref.shape[0]
    h = _rms(x_ref[...], g_ref[...]).astype(BF16)

    def proj(c0, n):
        return _dot(h, w_ref[:, c0:c0 + n])

    q = proj(0, sbw) * qscale
    k = proj(sbw, sbw).astype(BF16)
    v = proj(2 * sbw, sbw).astype(BF16)
    for c in range(ts // tq):
        rows = slice(c * tq, (c + 1) * tq)
        k_ref[rows, :] = _dot(perm_ref[...], k[rows, :]).astype(BF16)
        vp = _dot(perm_ref[...], v[rows, :])
        for hp in range(sbw // LANES):
            cols = slice(hp * LANES, (hp + 1) * LANES)
            qt_ref[hp, c] = q[rows, cols].T.astype(BF16)
            vt_ref[hp, c] = vp[:, cols].T.astype(BF16)
    u_ref[...] = proj(3 * sbw, ssw).astype(BF16)
    g0 = 3 * sbw + ssw
    cw = 512
    for c in range(d // cw):
        sl = slice(c * cw, (c + 1) * cw)
        ga_ref[:, sl] = jax.nn.sigmoid(proj(g0 + c * cw, cw) + bg_ref[:, sl]).astype(BF16)
        gs_ref[:, sl] = jax.nn.sigmoid(
            proj(g0 + d + c * cw, cw) + bg_ref[:, d + c * cw:d + (c + 1) * cw]).astype(BF16)


def _key_order(tk):
    r = np.arange(tk)
    return (r % SUBLANES) * (tk // SUBLANES) + r // SUBLANES


def _in_proj(x, gain, w_in, b_gate, *, sbw, ssw, ts, tq):
    b, s, d = x.shape
    in_w = w_in.shape[1]
    hp = sbw // LANES
    qscale = SB_HEAD_DIM ** -0.5 * LOG2E
    tok = lambda w: pl.BlockSpec((None, ts, w), lambda bi, ti: (bi, ti, 0))
    tr = pl.BlockSpec((None, hp, ts // tq, LANES, tq), lambda bi, ti: (bi, 0, ti, 0, 0))
    out_shape = (
        jax.ShapeDtypeStruct((b, hp, s // tq, LANES, tq), BF16),
        jax.ShapeDtypeStruct((b, s, sbw), BF16),
        jax.ShapeDtypeStruct((b, hp, s // tq, LANES, tq), BF16),
        jax.ShapeDtypeStruct((b, s, ssw), BF16),
        jax.ShapeDtypeStruct((b, s, d), BF16),
        jax.ShapeDtypeStruct((b, s, d), BF16),
    )
    perm = np.zeros((tq, tq), np.float32)
    perm[np.arange(tq), _key_order(tq)] = 1.0
    perm = jnp.asarray(perm, BF16)
    return pl.pallas_call(
        functools.partial(_inproj_kernel, sbw=sbw, ssw=ssw, d=d, qscale=qscale, tq=tq),
        grid=(b, s // ts),
        in_specs=[tok(d), _const_spec((1, d)), _const_spec((d, in_w)), _const_spec((1, 2 * d)),
                  _const_spec(perm.shape)],
        out_specs=(tr, tok(sbw), tr, tok(ssw), tok(d), tok(d)),
        out_shape=out_shape,
        compiler_params=_params(2),
        name="in_proj",
    )(x, gain, w_in, b_gate, perm)


_SB_DEPTH = 3
_SB_SLOTS = 2
_SB_UNROLL = 4
_SB_MASK_BIAS = -1e30


def _sb_item_table(nq):
    phases = [[(qi, qi) for qi in range(nq)],
              [(qi, qi - j) for qi in range(nq) for j in range(1, qi + 1)]]
    cols, bounds = [], [0]
    for items in phases:
        n = len(items)
        n_steps = -(-(n + _SB_DEPTH - 1) // _SB_UNROLL) * _SB_UNROLL if n else 0
        for t in range(n_steps):
            a = items[min(t, n - 1)]
            c = items[t - 1][0] if 0 <= t - 1 < n else nq
            d = items[t - 2] if 0 <= t - 2 < n else (nq, 0)
            cols.append((a[0], a[1], c, d[0], d[1]))
        bounds.append(len(cols))
    return np.asarray(cols, np.int32).T.copy(), bounds


def _sb_attn_kernel(tab_ref, qt_ref, k_ref, vt_ref, bias_ref, o_ref,
                    z_ref, w_ref, acc_ref, r_ref, *, tq, bounds):
    nq = qt_ref.shape[0]
    n_tiles = tq // SUBLANES
    drow = lax.broadcasted_iota(jnp.int32, (LANES, tq), 0)
    head_rows = [(drow // SB_HEAD_DIM) == h for h in range(2)]
    sub = lax.broadcasted_iota(jnp.int32, (SUBLANES, tq), 0)

    for ref in (z_ref, w_ref, acc_ref, r_ref):
        ref[...] = jnp.zeros_like(ref)

    def m1(t, s, diagonal):
        qt = qt_ref[tab_ref[0, t]]
        ks = pl.multiple_of(tab_ref[1, t] * tq, tq)
        kc = k_ref[pl.ds(ks, tq), :]
        for h in range(2):
            z = _dot(kc, jnp.where(head_rows[h], qt, jnp.zeros_like(qt)))
            z_ref[s, h] = z + bias_ref[...] if diagonal else z

    def v(t, s, diagonal):
        qi = tab_ref[2, t]
        for h in range(2):
            f = 1.0 / (1.0 + jnp.exp2(z_ref[s, h]))
            p = [None] * n_tiles
            p[-1] = f[(n_tiles - 1) * SUBLANES:, :]
            for i in range(n_tiles - 2, -1, -1):
                p[i] = f[i * SUBLANES:(i + 1) * SUBLANES, :] * p[i + 1]
            x = p[0]
            for step in (1, 2, 4):
                x = x * jnp.where(sub + step < SUBLANES, pltpu.roll(x, SUBLANES - step, 0), 1.0)
            g = jnp.where(sub + 1 < SUBLANES, pltpu.roll(x, SUBLANES - 1, 0), 1.0)
            total = jnp.broadcast_to(x[0:1, :], x.shape)
            if diagonal:
                r_ref[qi, h] = total
            else:
                r = r_ref[qi, h]
                g = g * r
                r_ref[qi, h] = r * total
            e = [p[i] * g for i in range(n_tiles)] + [g]
            w = jnp.concatenate([e[i + 1] - e[i] for i in range(n_tiles)], axis=0)
            w_ref[s, h] = w.astype(BF16)

    def m3(t, s, diagonal):
        qi = tab_ref[3, t]
        vc = vt_ref[tab_ref[4, t]]
        for h in range(2):
            pv = _dot(vc[h * SB_HEAD_DIM:(h + 1) * SB_HEAD_DIM, :], w_ref[s, h])
            acc_ref[qi, h] = pv if diagonal else acc_ref[qi, h] + pv

    def run_phase(first_step, last_step, diagonal):
        def outer(i, carry):
            for u in range(_SB_UNROLL):
                t = i * _SB_UNROLL + u
                cur, prev = u % _SB_SLOTS, (u - 1) % _SB_SLOTS
                m1(t, cur, diagonal)
                m3(t, cur, diagonal)
                v(t, prev, diagonal)
            return carry

        lax.fori_loop(first_step // _SB_UNROLL, last_step // _SB_UNROLL, outer, 0)

    run_phase(bounds[0], bounds[1], True)
    run_phase(bounds[1], bounds[2], False)
    for qi in range(nq):
        acc = jnp.concatenate([acc_ref[qi, 0], acc_ref[qi, 1]], axis=0)
        o_ref[qi * tq:(qi + 1) * tq, :] = acc.T.astype(o_ref.dtype)


def _sb_attn(qt, k, vt, *, tq):
    b, hp, nq, _, _ = qt.shape
    s = k.shape[1]
    tab, bounds = _sb_item_table(nq)
    key = _key_order(tq)[:, None]
    bias = jnp.asarray(np.where(key < np.arange(tq)[None, :], 0.0, _SB_MASK_BIAS), F32)
    tr = pl.BlockSpec((None, None, nq, LANES, tq), lambda bi, hi, tab: (bi, hi, 0, 0, 0))
    tokb = pl.BlockSpec((None, s, LANES), lambda bi, hi, tab: (bi, 0, hi))
    grid_spec = pltpu.PrefetchScalarGridSpec(
        num_scalar_prefetch=1,
        grid=(b, hp),
        in_specs=[tr, tokb, tr,
                  pl.BlockSpec(bias.shape, lambda bi, hi, tab: (0, 0), pipeline_mode=pl.Buffered(1))],
        out_specs=tokb,
        scratch_shapes=[
            pltpu.VMEM((_SB_SLOTS, 2, tq, tq), F32),
            pltpu.VMEM((_SB_SLOTS, 2, tq, tq), BF16),
            pltpu.VMEM((nq + 1, 2, SB_HEAD_DIM, tq), F32),
            pltpu.VMEM((nq + 1, 2, SUBLANES, tq), F32),
        ],
    )
    return pl.pallas_call(
        functools.partial(_sb_attn_kernel, tq=tq, bounds=tuple(bounds)),
        grid_spec=grid_spec,
        out_shape=jax.ShapeDtypeStruct((b, s, hp * LANES), BF16),
        compiler_params=_params(2),
        name="sb_attn",
    )(jnp.asarray(tab), qt, k, vt, bias)


def _ssm_kernel(u_ref, perm_ref, permt_ref, bs_ref, are_ref, aim_ref, cs_ref, d_ref, wg_ref, bg_ref,
                o_ref, bu_ref, xs_ref, *, t_steps, n_state):
    nb = u_ref.shape[0]
    n_slab = u_ref.shape[2] // LANES
    sw = n_state // n_slab

    @pl.when(pl.program_id(0) == 0)
    def _():
        xs_ref[...] = jnp.zeros_like(xs_ref)

    u_bt = u_ref[...].reshape(nb * t_steps, u_ref.shape[2])
    u = _dot(perm_ref[...], u_bt)
    ub = u.astype(BF16)
    for s in range(n_slab):
        bu = _dot(ub[:, s * LANES:(s + 1) * LANES], bs_ref[s])
        bu_ref[:, s * sw:(s + 1) * sw] = bu[:, :sw]
        bu_ref[:, n_state + s * sw:n_state + (s + 1) * sw] = bu[:, sw:]

    n_part = 2
    pw = n_state // n_part
    for part in range(n_part):
        re = slice(part * pw, (part + 1) * pw)
        im = slice(n_state + part * pw, n_state + (part + 1) * pw)
        a_re = are_ref[:, re]
        a_im = aim_ref[:, re]

        def body(t, carry, re=re, im=im, a_re=a_re, a_im=a_im):
            x_re, x_im = carry
            rows = pl.ds(pl.multiple_of(t * nb, nb), nb)
            n_re = a_re * x_re - a_im * x_im + bu_ref[rows, re]
            n_im = a_re * x_im + a_im * x_re + bu_ref[rows, im]
            bu_ref[rows, re] = n_re
            bu_ref[rows, im] = n_im
            return n_re, n_im

        x_re, x_im = lax.fori_loop(0, t_steps, body, (xs_ref[:, re], xs_ref[:, im]), unroll=2)
        xs_ref[:, re] = x_re
        xs_ref[:, im] = x_im

    ys = []
    for s in range(n_slab):
        x_re = bu_ref[:, s * sw:(s + 1) * sw].astype(BF16)
        x_im = bu_ref[:, n_state + s * sw:n_state + (s + 1) * sw].astype(BF16)
        ys.append(_dot(x_re, cs_ref[s, :sw, :]) + _dot(x_im, cs_ref[s, sw:, :]))
    y = jnp.concatenate(ys, axis=1) + d_ref[...] * u
    y = _gelu_tanh(y)
    gate = jax.nn.sigmoid(_dot(y.astype(BF16), wg_ref[...]) + bg_ref[...])
    out = _dot(permt_ref[...], (y * gate).astype(BF16))
    o_ref[...] = out.astype(o_ref.dtype).reshape(o_ref.shape)


def _ssm(u, b_slab, lam_re, lam_im, c_slab, d_skip, w_glu, b_glu, *, t_steps):
    batch, s, ssw = u.shape
    n_state = lam_re.shape[1]
    assert batch == SUBLANES, "the scan puts the batch on the sublanes"
    blk = t_steps * batch
    r = np.arange(blk)
    perm = np.zeros((blk, blk), np.float32)
    perm[r, (r % batch) * t_steps + r // batch] = 1.0
    perm = jnp.asarray(perm, BF16)
    tblk = pl.BlockSpec((batch, t_steps, ssw), lambda i: (0, i, 0))
    return pl.pallas_call(
        functools.partial(_ssm_kernel, t_steps=t_steps, n_state=n_state),
        grid=(s // t_steps,),
        in_specs=[
            tblk, _const_spec(perm.shape), _const_spec(perm.shape),
            _const_spec(b_slab.shape), _const_spec(lam_re.shape), _const_spec(lam_im.shape),
            _const_spec(c_slab.shape), _const_spec(d_skip.shape), _const_spec(w_glu.shape),
            _const_spec(b_glu.shape),
        ],
        out_specs=tblk,
        out_shape=jax.ShapeDtypeStruct((batch, s, ssw), BF16),
        scratch_shapes=[pltpu.VMEM((blk, 2 * n_state), F32),
                        pltpu.VMEM((batch, 2 * n_state), F32)],
        compiler_params=_params(1),
        name="ssm",
    )(u, perm, perm.T, b_slab, lam_re, lam_im, c_slab, d_skip, w_glu, b_glu)


def _cross_attention(x, gpre_ref, wq_ref, k_ref, v_ref, wo_ref, gpost_ref, heads):
    d = x.shape[1]
    hd = d // heads
    h = _rms(x, gpre_ref[...]).astype(BF16)
    q = (_dot(h, wq_ref[...]) * (hd ** -0.5)).astype(BF16)
    outs = []
    for i in range(heads):
        sl = slice(i * hd, (i + 1) * hd)
        s = _dot_nt(q[:, sl], k_ref[:, sl])
        p = jnp.exp(s - jnp.max(s, axis=-1, keepdims=True))
        inv = 1.0 / jnp.sum(p, axis=-1, keepdims=True)
        outs.append((_dot(p.astype(BF16), v_ref[:, sl]) * inv).astype(BF16))
    y = _dot(jnp.concatenate(outs, axis=1), wo_ref[...])
    return x + _rms(y, gpost_ref[...])


def _mix_xattn_kernel(oa_ref, os_ref, ga_ref, gs_ref, x_ref, wa_ref, ws_ref, wo_ref, gmix_ref,
                      gpre_ref, wq_ref, k_ref, v_ref, xwo_ref, gpost_ref, out_ref, *, heads):
    pa = _dot(oa_ref[...], wa_ref[...])
    ps = _dot(os_ref[...], ws_ref[...])
    merged = ga_ref[...].astype(F32) * pa + gs_ref[...].astype(F32) * ps
    y = _dot(merged.astype(BF16), wo_ref[...])
    x1 = x_ref[...] + _rms(y, gmix_ref[...])
    out_ref[...] = _cross_attention(x1, gpre_ref, wq_ref, k_ref, v_ref, xwo_ref, gpost_ref, heads)


def _mix_xattn(o_attn, o_ssm, ga, gs, x, wa, ws, wo, gmix, gpre, wq, kx, vx, xwo, gpost, *, ts):
    b, s, d = x.shape
    m = kx.shape[1]
    tok = lambda w: pl.BlockSpec((None, ts, w), lambda bi, ti: (bi, ti, 0))
    memb = pl.BlockSpec((None, m, d), lambda bi, ti: (bi, 0, 0))
    consts = (wa, ws, wo, gmix, gpre, wq)
    return pl.pallas_call(
        functools.partial(_mix_xattn_kernel, heads=XA_HEADS),
        grid=(b, s // ts),
        in_specs=[tok(o_attn.shape[2]), tok(o_ssm.shape[2]), tok(d), tok(d), tok(d)]
        + [_const_spec(a.shape) for a in consts]
        + [memb, memb, _const_spec(xwo.shape), _const_spec(gpost.shape)],
        out_specs=tok(d),
        out_shape=jax.ShapeDtypeStruct((b, s, d), F32),
        compiler_params=_params(2),
        name="mix_xattn",
    )(o_attn, o_ssm, ga, gs, x, *consts, kx, vx, xwo, gpost)


def _memkv_kernel(m_ref, g_ref, wk_ref, wv_ref, k_ref, v_ref):
    mn = _rms(m_ref[...], g_ref[...]).astype(BF16)
    k_ref[...] = _dot(mn, wk_ref[...]).astype(BF16)
    v_ref[...] = _dot(mn, wv_ref[...]).astype(BF16)


def _mem_kv(mem, gain, wk, wv):
    b, m, d = mem.shape
    blk = pl.BlockSpec((None, m, d), lambda bi: (bi, 0, 0))
    out = jax.ShapeDtypeStruct((b, m, d), BF16)
    return pl.pallas_call(
        _memkv_kernel,
        grid=(b,),
        in_specs=[blk, _const_spec(gain.shape), _const_spec(wk.shape), _const_spec(wv.shape)],
        out_specs=(blk, blk),
        out_shape=(out, out),
        compiler_params=_params(1),
        name="mem_kv",
    )(mem, gain, wk, wv)


def _ffn_kernel(x_ref, gpre_ref, wup_ref, cw_ref, cb_ref, wdn_ref, gpost_ref, out_ref,
                carry_ref, acc_ref, h_ref, up_ref, a_ref, *, d_ff, cwid):
    ts = x_ref.shape[0]
    halo = SUBLANES
    n_chunk = d_ff // cwid

    @pl.when(pl.program_id(1) == 0)
    def _():
        carry_ref[...] = jnp.zeros_like(carry_ref)

    h_ref[...] = _rms(x_ref[...], gpre_ref[...]).astype(BF16)

    def stage_up(c, slot):
        for half in range(2):
            c0 = half * d_ff + c * cwid
            up_ref[slot, half] = _dot(h_ref[...], wup_ref[:, c0:c0 + cwid])

    def conv(c0, up):
        ext = jnp.concatenate([carry_ref[:, c0:c0 + cwid], up], axis=0)
        carry_ref[:, c0:c0 + cwid] = up[ts - halo:, :]
        out = cb_ref[:, c0:c0 + cwid] + cw_ref[CONV_WIDTH - 1:CONV_WIDTH, c0:c0 + cwid] * up
        for back in range(1, CONV_WIDTH):
            tap = pltpu.roll(ext, back, 0)[halo:, :]
            out = out + cw_ref[CONV_WIDTH - 1 - back:CONV_WIDTH - back, c0:c0 + cwid] * tap
        return out

    def stage_act(c, slot):
        gate = conv(c * cwid, up_ref[slot, 0])
        val = conv(d_ff + c * cwid, up_ref[slot, 1])
        a_ref[slot] = (_gelu_tanh(gate) * val).astype(BF16)

    def stage_down(c, slot):
        part = _dot(a_ref[slot], wdn_ref[c * cwid:(c + 1) * cwid, :])
        acc_ref[...] = part if c == 0 else acc_ref[...] + part

    for i in range(n_chunk + 2):
        if i >= 2:
            stage_down(i - 2, i % 2)
        if 1 <= i <= n_chunk:
            stage_act(i - 1, (i - 1) % 2)
        if i < n_chunk:
            stage_up(i, i % 2)
    out_ref[...] = x_ref[...] + _rms(acc_ref[...], gpost_ref[...])


def _conv_ffn(x, gpre, w_up, conv_w, conv_b, w_dn, gpost, *, ts, cwid):
    b, s, d = x.shape
    d_ff = w_dn.shape[0]
    tok = pl.BlockSpec((None, ts, d), lambda bi, ti: (bi, ti, 0))
    return pl.pallas_call(
        functools.partial(_ffn_kernel, d_ff=d_ff, cwid=cwid),
        grid=(b, s // ts),
        in_specs=[tok, _const_spec(gpre.shape), _const_spec(w_up.shape), _const_spec(conv_w.shape),
                  _const_spec(conv_b.shape), _const_spec(w_dn.shape), _const_spec(gpost.shape)],
        out_specs=tok,
        out_shape=jax.ShapeDtypeStruct((b, s, d), F32),
        scratch_shapes=[pltpu.VMEM((SUBLANES, 2 * d_ff), F32),
                        pltpu.VMEM((ts, d), F32),
                        pltpu.VMEM((ts, d), BF16),
                        pltpu.VMEM((2, 2, ts, cwid), F32),
                        pltpu.VMEM((2, ts, cwid), BF16)],
        compiler_params=_params(2),
        name="conv_ffn",
    )(x, gpre, w_up, conv_w, conv_b, w_dn, gpost)


def _ssm_slabs(bb_re, bb_im, c_re, c_im):
    g, p, c = bb_re.shape
    gps = LANES // c
    n_slab = g // gps
    eye = jnp.eye(gps, dtype=F32)

    def b_part(bb):
        t = bb.reshape(n_slab, gps, p, c)
        return jnp.einsum('sgpc,gh->sgchp', t, eye).reshape(n_slab, gps * c, gps * p)

    def c_part(cc):
        t = cc.reshape(n_slab, gps, c, p)
        return jnp.einsum('sgcp,gh->shpgc', t, eye).reshape(n_slab, gps * p, gps * c)

    b_slab = jnp.concatenate([b_part(bb_re), b_part(bb_im)], axis=2).astype(BF16)
    c_slab = jnp.concatenate([c_part(c_re), c_part(-c_im)], axis=1).astype(BF16)
    return b_slab, c_slab


def _layer(x, mem, l, p):
    b, s, d = x.shape
    sbw = p["w_branch_attn"].shape[1]
    ssw = p["w_branch_ssm"].shape[1]
    row = lambda a: a[l][None, :].astype(F32)
    wb = lambda a: a[l].astype(BF16)

    lam_re, lam_im, bb_re, bb_im = _discretize(
        p["ssm_a_re"][l], p["ssm_a_im"][l], p["ssm_log_dt"][l], p["ssm_b_re"][l], p["ssm_b_im"][l])
    b_slab, c_slab = _ssm_slabs(bb_re, bb_im, p["ssm_c_re"][l].astype(F32), p["ssm_c_im"][l].astype(F32))
    n_state = lam_re.size
    lam_re_b = jnp.broadcast_to(lam_re.reshape(1, n_state), (SUBLANES, n_state))
    lam_im_b = jnp.broadcast_to(lam_im.reshape(1, n_state), (SUBLANES, n_state))

    qt, k, vt, u, ga, gs = _in_proj(x, row(p["norm_mix_pre"]), wb(p["w_in"]), row(p["b_gate"]),
                                       sbw=sbw, ssw=ssw, ts=512, tq=256)
    o_attn = _sb_attn(qt, k, vt, tq=256)
    o_ssm = _ssm(u, b_slab, lam_re_b, lam_im_b, c_slab,
                 row(p["ssm_d"]), wb(p["ssm_w_glu"]), row(p["ssm_b_glu"]), t_steps=64)
    kx, vx = _mem_kv(mem, row(p["norm_mem"]), wb(p["xa_wk"]), wb(p["xa_wv"]))
    x = _mix_xattn(o_attn, o_ssm, ga, gs, x, wb(p["w_branch_attn"]), wb(p["w_branch_ssm"]),
                   wb(p["w_out"]), row(p["norm_mix_post"]), row(p["norm_xa_pre"]), wb(p["xa_wq"]),
                   kx, vx, wb(p["xa_wo"]), row(p["norm_xa_post"]), ts=512)

    x = _conv_ffn(x, row(p["norm_ffn_pre"]), wb(p["ffn_w_up"]), p["ffn_conv_w"][l].astype(F32),
                  row(p["ffn_conv_b"]), wb(p["ffn_w_down"]), row(p["norm_ffn_post"]), ts=512, cwid=256)
    return x


def kernel(x, mem, norm_mix_pre, norm_mix_post, w_in, b_gate, ssm_a_re, ssm_a_im, ssm_log_dt, ssm_b_re, ssm_b_im, ssm_c_re, ssm_c_im, ssm_d, ssm_w_glu, ssm_b_glu, w_branch_attn, w_branch_ssm, w_out, norm_xa_pre, norm_xa_post, norm_mem, xa_wq, xa_wk, xa_wv, xa_wo, norm_ffn_pre, norm_ffn_post, ffn_w_up, ffn_conv_w, ffn_conv_b, ffn_w_down):
    p = dict(norm_mix_pre=norm_mix_pre, norm_mix_post=norm_mix_post, w_in=w_in, b_gate=b_gate,
             ssm_a_re=ssm_a_re, ssm_a_im=ssm_a_im, ssm_log_dt=ssm_log_dt, ssm_b_re=ssm_b_re,
             ssm_b_im=ssm_b_im, ssm_c_re=ssm_c_re, ssm_c_im=ssm_c_im, ssm_d=ssm_d,
             ssm_w_glu=ssm_w_glu, ssm_b_glu=ssm_b_glu, w_branch_attn=w_branch_attn,
             w_branch_ssm=w_branch_ssm, w_out=w_out, norm_xa_pre=norm_xa_pre,
             norm_xa_post=norm_xa_post, norm_mem=norm_mem, xa_wq=xa_wq, xa_wk=xa_wk, xa_wv=xa_wv,
             xa_wo=xa_wo, norm_ffn_pre=norm_ffn_pre, norm_ffn_post=norm_ffn_post,
             ffn_w_up=ffn_w_up, ffn_conv_w=ffn_conv_w, ffn_conv_b=ffn_conv_b, ffn_w_down=ffn_w_down)
    for l in range(w_in.shape[0]):
        x = _layer(x, mem, l, p)
    return x
```

```python
import functools
import math

import numpy as np

import jax
import jax.numpy as jnp
from jax import lax
from jax.experimental import pallas as pl
from jax.experimental.pallas import tpu as pltpu

F32 = jnp.float32
BF16 = jnp.bfloat16

RMS_EPS = 1e-6
SB_HEAD_DIM = 64
SSM_GROUP = 16
SSM_STATE = 64
XA_HEADS = 4
CONV_WIDTH = 3
LANES = 128
SUBLANES = 8
VMEM_LIMIT = 56 * 1024 * 1024
LOG2E = 1.4426950408889634


def _rms(x, g):
    ms = jnp.mean(x * x, axis=-1, keepdims=True)
    return x * lax.rsqrt(ms + RMS_EPS) * g


def _dot(a, b):
    return jnp.dot(a, b, preferred_element_type=F32)


def _gelu_tanh(x):
    a = -2.0 * math.sqrt(2.0 / math.pi) * LOG2E
    return x / (1.0 + jnp.exp2(x * (a + (a * 0.044715) * (x * x))))


def _dot_nt(a, b):
    return lax.dot_general(a, b, (((1,), (1,)), ((), ())), preferred_element_type=F32)


def _params(n_axes):
    return pltpu.CompilerParams(
        dimension_semantics=("arbitrary",) * n_axes, vmem_limit_bytes=VMEM_LIMIT)


def _const_spec(shape):
    zeros = (0,) * len(shape)
    return pl.BlockSpec(shape, lambda *_: zeros, pipeline_mode=pl.Buffered(1))


def _discretize_kernel(are_ref, aim_ref, ldt_ref, bre_ref, bim_ref,
                       lre_ref, lim_ref, bbre_ref, bbim_ref):
    a_re = are_ref[...]
    a_im = aim_ref[...]
    dt = jnp.exp(ldt_ref[...])
    mag = jnp.exp(a_re * dt)
    l_re = mag * jnp.cos(a_im * dt)
    l_im = mag * jnp.sin(a_im * dt)
    lre_ref[...] = l_re
    lim_ref[...] = l_im
    n_re = l_re - 1.0
    inv = 1.0 / (a_re * a_re + a_im * a_im)
    c_re = (n_re * a_re + l_im * a_im) * inv
    c_im = (l_im * a_re - n_re * a_im) * inv
    b_re = bre_ref[...]
    b_im = bim_ref[...]
    bbre_ref[...] = c_re * b_re - c_im * b_im
    bbim_ref[...] = c_re * b_im + c_im * b_re


def _discretize(a_re, a_im, log_dt, b_re, b_im):
    g, p, c = b_re.shape
    rep = lambda a: jnp.repeat(a, c, axis=-1)
    args = (rep(a_re), rep(a_im), jnp.broadcast_to(log_dt[:, None], (g, p * c)),
            b_re.reshape(g, p * c), b_im.reshape(g, p * c))
    out = jax.ShapeDtypeStruct((g, p * c), F32)
    l_re, l_im, bb_re, bb_im = pl.pallas_call(
        _discretize_kernel, out_shape=(out, out, out, out), name="ssm_discretize")(*args)
    return (l_re[:, ::c], l_im[:, ::c], bb_re.reshape(g, p, c), bb_im.reshape(g, p, c))


def _inproj_kernel(x_ref, g_ref, w_ref, bg_ref, perm_ref, qt_ref, k_ref, vt_ref, u_ref, ga_ref, gs_ref,
                   *, sbw, ssw, d, qscale, tq):
    ts = x_ref.shape[0]
    h = _rms(x_ref[...], g_ref[...]).astype(BF16)

    def proj(c0, n):
        return _dot(h, w_ref[:, c0:c0 + n])

    q = proj(0, sbw) * qscale
    k = proj(sbw, sbw).astype(BF16)
    v = proj(2 * sbw, sbw).astype(BF16)
    for c in range(ts // tq):
        rows = slice(c * tq, (c + 1) * tq)
        k_ref[rows, :] = _dot(perm_ref[...], k[rows, :]).astype(BF16)
        vp = _dot(perm_ref[...], v[rows, :])
        for hp in range(sbw // LANES):
            cols = slice(hp * LANES, (hp + 1) * LANES)
            qt_ref[hp, c] = q[rows, cols].T.astype(BF16)
            vt_ref[hp, c] = vp[:, cols].T.astype(BF16)
    u_ref[...] = proj(3 * sbw, ssw).astype(BF16)
    g0 = 3 * sbw + ssw
    cw = 512
    for c in range(d // cw):
        sl = slice(c * cw, (c + 1) * cw)
        ga_ref[:, sl] = jax.nn.sigmoid(proj(g0 + c * cw, cw) + bg_ref[:, sl]).astype(BF16)
        gs_ref[:, sl] = jax.nn.sigmoid(
            proj(g0 + d + c * cw, cw) + bg_ref[:, d + c * cw:d + (c + 1) * cw]).astype(BF16)


def _key_order(tk):
    r = np.arange(tk)
    return (r % SUBLANES) * (tk // SUBLANES) + r // SUBLANES


def _in_proj(x, gain, w_in, b_gate, *, sbw, ssw, ts, tq):
    b, s, d = x.shape
    in_w = w_in.shape[1]
    hp = sbw // LANES
    qscale = SB_HEAD_DIM ** -0.5 * LOG2E
    tok = lambda w: pl.BlockSpec((None, ts, w), lambda bi, ti: (bi, ti, 0))
    tr = pl.BlockSpec((None, hp, ts // tq, LANES, tq), lambda bi, ti: (bi, 0, ti, 0, 0))
    out_shape = (
        jax.ShapeDtypeStruct((b, hp, s // tq, LANES, tq), BF16),
        jax.ShapeDtypeStruct((b, s, sbw), BF16),
        jax.ShapeDtypeStruct((b, hp, s // tq, LANES, tq), BF16),
        jax.ShapeDtypeStruct((b, s, ssw), BF16),
        jax.ShapeDtypeStruct((b, s, d), BF16),
        jax.ShapeDtypeStruct((b, s, d), BF16),
    )
    perm = np.zeros((tq, tq), np.float32)
    perm[np.arange(tq), _key_order(tq)] = 1.0
    perm = jnp.asarray(perm, BF16)
    return pl.pallas_call(
        functools.partial(_inproj_kernel, sbw=sbw, ssw=ssw, d=d, qscale=qscale, tq=tq),
        grid=(b, s // ts),
        in_specs=[tok(d), _const_spec((1, d)), _const_spec((d, in_w)), _const_spec((1, 2 * d)),
                  _const_spec(perm.shape)],
        out_specs=(tr, tok(sbw), tr, tok(ssw), tok(d), tok(d)),
        out_shape=out_shape,
        compiler_params=_params(2),
        name="in_proj",
    )(x, gain, w_in, b_gate, perm)


_SB_DEPTH = 3
_SB_SLOTS = 2
_SB_UNROLL = 6
_SB_MASK_BIAS = -1e30


def _sb_item_table(nq):
    phases = [[(qi, qi) for qi in range(nq)],
              [(qi, qi - j) for qi in range(nq) for j in range(1, qi + 1)]]
    cols, bounds = [], [0]
    for items in phases:
        n = len(items)
        n_steps = -(-(n + _SB_DEPTH - 1) // _SB_UNROLL) * _SB_UNROLL if n else 0
        for t in range(n_steps):
            a = items[min(t, n - 1)]
            c = items[t - 1][0] if 0 <= t - 1 < n else nq
            d = items[t - 2] if 0 <= t - 2 < n else (nq, 0)
            cols.append((a[0], a[1], c, d[0], d[1]))
        bounds.append(len(cols))
    return np.asarray(cols, np.int32).T.copy(), bounds


def _sb_attn_kernel(tab_ref, qt_ref, k_ref, vt_ref, bias_ref, o_ref,
                    z_ref, w_ref, acc_ref, r_ref, *, tq, bounds):
    nq = qt_ref.shape[0]
    n_tiles = tq // SUBLANES
    drow = lax.broadcasted_iota(jnp.int32, (LANES, tq), 0)
    head_rows = [(drow // SB_HEAD_DIM) == h for h in range(2)]
    sub = lax.broadcasted_iota(jnp.int32, (SUBLANES, tq), 0)

    for ref in (z_ref, w_ref, acc_ref, r_ref):
        ref[...] = jnp.zeros_like(ref)

    def m1(t, s, diagonal):
        qt = qt_ref[tab_ref[0, t]]
        ks = pl.multiple_of(tab_ref[1, t] * tq, tq)
        kc = k_ref[pl.ds(ks, tq), :]
        for h in range(2):
            z = _dot(kc, jnp.where(head_rows[h], qt, jnp.zeros_like(qt)))
            z_ref[s, h] = z + bias_ref[...] if diagonal else z

    def v(t, s, diagonal):
        qi = tab_ref[2, t]
        for h in range(2):
            f = 1.0 / (1.0 + jnp.exp2(z_ref[s, h]))
            p = [None] * n_tiles
            p[-1] = f[(n_tiles - 1) * SUBLANES:, :]
            for i in range(n_tiles - 2, -1, -1):
                p[i] = f[i * SUBLANES:(i + 1) * SUBLANES, :] * p[i + 1]
            x = p[0]
            for step in (1, 2, 4):
                x = x * jnp.where(sub + step < SUBLANES, pltpu.roll(x, SUBLANES - step, 0), 1.0)
            g = jnp.where(sub + 1 < SUBLANES, pltpu.roll(x, SUBLANES - 1, 0), 1.0)
            total = jnp.broadcast_to(x[0:1, :], x.shape)
            if diagonal:
                r_ref[qi, h] = total
            else:
                r = r_ref[qi, h]
                g = g * r
                r_ref[qi, h] = r * total
            e = [p[i] * g for i in range(n_tiles)] + [g]
            w = jnp.concatenate([e[i + 1] - e[i] for i in range(n_tiles)], axis=0)
            w_ref[s, h] = w.astype(BF16)

    def m3(t, s, diagonal):
        qi = tab_ref[3, t]
        vc = vt_ref[tab_ref[4, t]]
        for h in range(2):
            pv = _dot(vc[h * SB_HEAD_DIM:(h + 1) * SB_HEAD_DIM, :], w_ref[s, h])
            acc_ref[qi, h] = pv if diagonal else acc_ref[qi, h] + pv

    def run_phase(first_step, last_step, diagonal):
        def outer(i, carry):
            for u in range(_SB_UNROLL):
                t = i * _SB_UNROLL + u
                cur, prev = u % _SB_SLOTS, (u - 1) % _SB_SLOTS
                m1(t, cur, diagonal)
                m3(t, cur, diagonal)
                v(t, prev, diagonal)
            return carry

        lax.fori_loop(first_step // _SB_UNROLL, last_step // _SB_UNROLL, outer, 0)

    run_phase(bounds[0], bounds[1], True)
    run_phase(bounds[1], bounds[2], False)
    for qi in range(nq):
        acc = jnp.concatenate([acc_ref[qi, 0], acc_ref[qi, 1]], axis=0)
        o_ref[qi * tq:(qi + 1) * tq, :] = acc.T.astype(o_ref.dtype)


def _sb_attn(qt, k, vt, *, tq):
    b, hp, nq, _, _ = qt.shape
    s = k.shape[1]
    tab, bounds = _sb_item_table(nq)
    key = _key_order(tq)[:, None]
    bias = jnp.asarray(np.where(key < np.arange(tq)[None, :], 0.0, _SB_MASK_BIAS), F32)
    tr = pl.BlockSpec((None, None, nq, LANES, tq), lambda bi, hi, tab: (bi, hi, 0, 0, 0))
    tokb = pl.BlockSpec((None, s, LANES), lambda bi, hi, tab: (bi, 0, hi))
    grid_spec = pltpu.PrefetchScalarGridSpec(
        num_scalar_prefetch=1,
        grid=(b, hp),
        in_specs=[tr, tokb, tr,
                  pl.BlockSpec(bias.shape, lambda bi, hi, tab: (0, 0), pipeline_mode=pl.Buffered(1))],
        out_specs=tokb,
        scratch_shapes=[
            pltpu.VMEM((_SB_SLOTS, 2, tq, tq), F32),
            pltpu.VMEM((_SB_SLOTS, 2, tq, tq), BF16),
            pltpu.VMEM((nq + 1, 2, SB_HEAD_DIM, tq), F32),
            pltpu.VMEM((nq + 1, 2, SUBLANES, tq), F32),
        ],
    )
    return pl.pallas_call(
        functools.partial(_sb_attn_kernel, tq=tq, bounds=tuple(bounds)),
        grid_spec=grid_spec,
        out_shape=jax.ShapeDtypeStruct((b, s, hp * LANES), BF16),
        compiler_params=_params(2),
        name="sb_attn",
    )(jnp.asarray(tab), qt, k, vt, bias)


def _ssm_kernel(u_ref, perm_ref, permt_ref, bs_ref, are_ref, aim_ref, cs_ref, d_ref, wg_ref, bg_ref,
                o_ref, bu_ref, xs_ref, *, t_steps, n_state):
    nb = u_ref.shape[0]
    n_slab = u_ref.shape[2] // LANES
    sw = n_state // n_slab

    @pl.when(pl.program_id(0) == 0)
    def _():
        xs_ref[...] = jnp.zeros_like(xs_ref)

    u_bt = u_ref[...].reshape(nb * t_steps, u_ref.shape[2])
    u = _dot(perm_ref[...], u_bt)
    ub = u.astype(BF16)
    for s in range(n_slab):
        bu = _dot(ub[:, s * LANES:(s + 1) * LANES], bs_ref[s])
        bu_ref[:, s * sw:(s + 1) * sw] = bu[:, :sw]
        bu_ref[:, n_state + s * sw:n_state + (s + 1) * sw] = bu[:, sw:]

    n_part = 2
    pw = n_state // n_part
    for part in range(n_part):
        re = slice(part * pw, (part + 1) * pw)
        im = slice(n_state + part * pw, n_state + (part + 1) * pw)
        a_re = are_ref[:, re]
        a_im = aim_ref[:, re]

        def body(t, carry, re=re, im=im, a_re=a_re, a_im=a_im):
            x_re, x_im = carry
            rows = pl.ds(pl.multiple_of(t * nb, nb), nb)
            n_re = a_re * x_re - a_im * x_im + bu_ref[rows, re]
            n_im = a_re * x_im + a_im * x_re + bu_ref[rows, im]
            bu_ref[rows, re] = n_re
            bu_ref[rows, im] = n_im
            return n_re, n_im

        x_re, x_im = lax.fori_loop(0, t_steps, body, (xs_ref[:, re], xs_ref[:, im]), unroll=2)
        xs_ref[:, re] = x_re
        xs_ref[:, im] = x_im

    ys = []
    for s in range(n_slab):
        x_re = bu_ref[:, s * sw:(s + 1) * sw].astype(BF16)
        x_im = bu_ref[:, n_state + s * sw:n_state + (s + 1) * sw].astype(BF16)
        ys.append(_dot(x_re, cs_ref[s, :sw, :]) + _dot(x_im, cs_ref[s, sw:, :]))
    y = jnp.concatenate(ys, axis=1) + d_ref[...] * u
    y = _gelu_tanh(y)
    gate = jax.nn.sigmoid(_dot(y.astype(BF16), wg_ref[...]) + bg_ref[...])
    out = _dot(permt_ref[...], (y * gate).astype(BF16))
    o_ref[...] = out.astype(o_ref.dtype).reshape(o_ref.shape)


def _ssm(u, b_slab, lam_re, lam_im, c_slab, d_skip, w_glu, b_glu, *, t_steps):
    batch, s, ssw = u.shape
    n_state = lam_re.shape[1]
    assert batch == SUBLANES, "the scan puts the batch on the sublanes"
    blk = t_steps * batch
    r = np.arange(blk)
    perm = np.zeros((blk, blk), np.float32)
    perm[r, (r % batch) * t_steps + r // batch] = 1.0
    perm = jnp.asarray(perm, BF16)
    tblk = pl.BlockSpec((batch, t_steps, ssw), lambda i: (0, i, 0))
    return pl.pallas_call(
        functools.partial(_ssm_kernel, t_steps=t_steps, n_state=n_state),
        grid=(s // t_steps,),
        in_specs=[
            tblk, _const_spec(perm.shape), _const_spec(perm.shape),
            _const_spec(b_slab.shape), _const_spec(lam_re.shape), _const_spec(lam_im.shape),
            _const_spec(c_slab.shape), _const_spec(d_skip.shape), _const_spec(w_glu.shape),
            _const_spec(b_glu.shape),
        ],
        out_specs=tblk,
        out_shape=jax.ShapeDtypeStruct((batch, s, ssw), BF16),
        scratch_shapes=[pltpu.VMEM((blk, 2 * n_state), F32),
                        pltpu.VMEM((batch, 2 * n_state), F32)],
        compiler_params=_params(1),
        name="ssm",
    )(u, perm, perm.T, b_slab, lam_re, lam_im, c_slab, d_skip, w_glu, b_glu)


def _cross_attention(x, gpre_ref, wq_ref, k_ref, v_ref, wo_ref, gpost_ref, heads):
    d = x.shape[1]
    hd = d // heads
    h = _rms(x, gpre_ref[...]).astype(BF16)
    q = (_dot(h, wq_ref[...]) * (hd ** -0.5)).astype(BF16)
    outs = []
    for i in range(heads):
        sl = slice(i * hd, (i + 1) * hd)
        s = _dot_nt(q[:, sl], k_ref[:, sl])
        p = jnp.exp(s - jnp.max(s, axis=-1, keepdims=True))
        inv = 1.0 / jnp.sum(p, axis=-1, keepdims=True)
        outs.append((_dot(p.astype(BF16), v_ref[:, sl]) * inv).astype(BF16))
    y = _dot(jnp.concatenate(outs, axis=1), wo_ref[...])
    return x + _rms(y, gpost_ref[...])


def _mix_xattn_kernel(oa_ref, os_ref, ga_ref, gs_ref, x_ref, wa_ref, ws_ref, wo_ref, gmix_ref,
                      gpre_ref, wq_ref, k_ref, v_ref, xwo_ref, gpost_ref, out_ref, *, heads):
    pa = _dot(oa_ref[...], wa_ref[...])
    ps = _dot(os_ref[...], ws_ref[...])
    merged = ga_ref[...].astype(F32) * pa + gs_ref[...].astype(F32) * ps
    y = _dot(merged.astype(BF16), wo_ref[...])
    x1 = x_ref[...] + _rms(y, gmix_ref[...])
    out_ref[...] = _cross_attention(x1, gpre_ref, wq_ref, k_ref, v_ref, xwo_ref, gpost_ref, heads)


def _mix_xattn(o_attn, o_ssm, ga, gs, x, wa, ws, wo, gmix, gpre, wq, kx, vx, xwo, gpost, *, ts):
    b, s, d = x.shape
    m = kx.shape[1]
    tok = lambda w: pl.BlockSpec((None, ts, w), lambda bi, ti: (bi, ti, 0))
    memb = pl.BlockSpec((None, m, d), lambda bi, ti: (bi, 0, 0))
    consts = (wa, ws, wo, gmix, gpre, wq)
    return pl.pallas_call(
        functools.partial(_mix_xattn_kernel, heads=XA_HEADS),
        grid=(b, s // ts),
        in_specs=[tok(o_attn.shape[2]), tok(o_ssm.shape[2]), tok(d), tok(d), tok(d)]
        + [_const_spec(a.shape) for a in consts]
        + [memb, memb, _const_spec(xwo.shape), _const_spec(gpost.shape)],
        out_specs=tok(d),
        out_shape=jax.ShapeDtypeStruct((b, s, d), F32),
        compiler_params=_params(2),
        name="mix_xattn",
    )(o_attn, o_ssm, ga, gs, x, *consts, kx, vx, xwo, gpost)


def _memkv_kernel(m_ref, g_ref, wk_ref, wv_ref, k_ref, v_ref):
    mn = _rms(m_ref[...], g_ref[...]).astype(BF16)
    k_ref[...] = _dot(mn, wk_ref[...]).astype(BF16)
    v_ref[...] = _dot(mn, wv_ref[...]).astype(BF16)


def _mem_kv(mem, gain, wk, wv):
    b, m, d = mem.shape
    blk = pl.BlockSpec((None, m, d), lambda bi: (bi, 0, 0))
    out = jax.ShapeDtypeStruct((b, m, d), BF16)
    return pl.pallas_call(
        _memkv_kernel,
        grid=(b,),
        in_specs=[blk, _const_spec(gain.shape), _const_spec(wk.shape), _const_spec(wv.shape)],
        out_specs=(blk, blk),
        out_shape=(out, out),
        compiler_params=_params(1),
        name="mem_kv",
    )(mem, gain, wk, wv)


def _ffn_kernel(x_ref, gpre_ref, wup_ref, cw_ref, cb_ref, wdn_ref, gpost_ref, out_ref,
                carry_ref, acc_ref, h_ref, up_ref, a_ref, *, d_ff, cwid):
    ts = x_ref.shape[0]
    halo = SUBLANES
    n_chunk = d_ff // cwid

    @pl.when(pl.program_id(1) == 0)
    def _():
        carry_ref[...] = jnp.zeros_like(carry_ref)

    h_ref[...] = _rms(x_ref[...], gpre_ref[...]).astype(BF16)

    def stage_up(c, slot):
        for half in range(2):
            c0 = half * d_ff + c * cwid
            up_ref[slot, half] = _dot(h_ref[...], wup_ref[:, c0:c0 + cwid])

    def conv(c0, up):
        ext = jnp.concatenate([carry_ref[:, c0:c0 + cwid], up], axis=0)
        carry_ref[:, c0:c0 + cwid] = up[ts - halo:, :]
        out = cb_ref[:, c0:c0 + cwid] + cw_ref[CONV_WIDTH - 1:CONV_WIDTH, c0:c0 + cwid] * up
        for back in range(1, CONV_WIDTH):
            tap = pltpu.roll(ext, back, 0)[halo:, :]
            out = out + cw_ref[CONV_WIDTH - 1 - back:CONV_WIDTH - back, c0:c0 + cwid] * tap
        return out

    def stage_act(c, slot):
        gate = conv(c * cwid, up_ref[slot, 0])
        val = conv(d_ff + c * cwid, up_ref[slot, 1])
        a_ref[slot] = (_gelu_tanh(gate) * val).astype(BF16)

    def stage_down(c, slot):
        part = _dot(a_ref[slot], wdn_ref[c * cwid:(c + 1) * cwid, :])
        acc_ref[...] = part if c == 0 else acc_ref[...] + part

    for i in range(n_chunk + 2):
        if i >= 2:
            stage_down(i - 2, i % 2)
        if 1 <= i <= n_chunk:
            stage_act(i - 1, (i - 1) % 2)
        if i < n_chunk:
            stage_up(i, i % 2)
    out_ref[...] = x_ref[...] + _rms(acc_ref[...], gpost_ref[...])


def _conv_ffn(x, gpre, w_up, conv_w, conv_b, w_dn, gpost, *, ts, cwid):
    b, s, d = x.shape
    d_ff = w_dn.shape[0]
    tok = pl.BlockSpec((None, ts, d), lambda bi, ti: (bi, ti, 0))
    return pl.pallas_call(
        functools.partial(_ffn_kernel, d_ff=d_ff, cwid=cwid),
        grid=(b, s // ts),
        in_specs=[tok, _const_spec(gpre.shape), _const_spec(w_up.shape), _const_spec(conv_w.shape),
                  _const_spec(conv_b.shape), _const_spec(w_dn.shape), _const_spec(gpost.shape)],
        out_specs=tok,
        out_shape=jax.ShapeDtypeStruct((b, s, d), F32),
        scratch_shapes=[pltpu.VMEM((SUBLANES, 2 * d_ff), F32),
                        pltpu.VMEM((ts, d), F32),
                        pltpu.VMEM((ts, d), BF16),
                        pltpu.VMEM((2, 2, ts, cwid), F32),
                        pltpu.VMEM((2, ts, cwid), BF16)],
        compiler_params=_params(2),
        name="conv_ffn",
    )(x, gpre, w_up, conv_w, conv_b, w_dn, gpost)


def _ssm_slabs(bb_re, bb_im, c_re, c_im):
    g, p, c = bb_re.shape
    gps = LANES // c
    n_slab = g // gps
    eye = jnp.eye(gps, dtype=F32)

    def b_part(bb):
        t = bb.reshape(n_slab, gps, p, c)
        return jnp.einsum('sgpc,gh->sgchp', t, eye).reshape(n_slab, gps * c, gps * p)

    def c_part(cc):
        t = cc.reshape(n_slab, gps, c, p)
        return jnp.einsum('sgcp,gh->shpgc', t, eye).reshape(n_slab, gps * p, gps * c)

    b_slab = jnp.concatenate([b_part(bb_re), b_part(bb_im)], axis=2).astype(BF16)
    c_slab = jnp.concatenate([c_part(c_re), c_part(-c_im)], axis=1).astype(BF16)
    return b_slab, c_slab


def _layer(x, mem, l, p):
    b, s, d = x.shape
    sbw = p["w_branch_attn"].shape[1]
    ssw = p["w_branch_ssm"].shape[1]
    row = lambda a: a[l][None, :].astype(F32)
    wb = lambda a: a[l].astype(BF16)

    lam_re, lam_im, bb_re, bb_im = _discretize(
        p["ssm_a_re"][l], p["ssm_a_im"][l], p["ssm_log_dt"][l], p["ssm_b_re"][l], p["ssm_b_im"][l])
    b_slab, c_slab = _ssm_slabs(bb_re, bb_im, p["ssm_c_re"][l].astype(F32), p["ssm_c_im"][l].astype(F32))
    n_state = lam_re.size
    lam_re_b = jnp.broadcast_to(lam_re.reshape(1, n_state), (SUBLANES, n_state))
    lam_im_b = jnp.broadcast_to(lam_im.reshape(1, n_state), (SUBLANES, n_state))

    qt, k, vt, u, ga, gs = _in_proj(x, row(p["norm_mix_pre"]), wb(p["w_in"]), row(p["b_gate"]),
                                       sbw=sbw, ssw=ssw, ts=512, tq=256)
    o_attn = _sb_attn(qt, k, vt, tq=256)
    o_ssm = _ssm(u, b_slab, lam_re_b, lam_im_b, c_slab,
                 row(p["ssm_d"]), wb(p["ssm_w_glu"]), row(p["ssm_b_glu"]), t_steps=64)
    kx, vx = _mem_kv(mem, row(p["norm_mem"]), wb(p["xa_wk"]), wb(p["xa_wv"]))
    x = _mix_xattn(o_attn, o_ssm, ga, gs, x, wb(p["w_branch_attn"]), wb(p["w_branch_ssm"]),
                   wb(p["w_out"]), row(p["norm_mix_post"]), row(p["norm_xa_pre"]), wb(p["xa_wq"]),
                   kx, vx, wb(p["xa_wo"]), row(p["norm_xa_post"]), ts=512)

    x = _conv_ffn(x, row(p["norm_ffn_pre"]), wb(p["ffn_w_up"]), p["ffn_conv_w"][l].astype(F32),
                  row(p["ffn_conv_b"]), wb(p["ffn_w_down"]), row(p["norm_ffn_post"]), ts=512, cwid=256)
    return x


def kernel(x, mem, norm_mix_pre, norm_mix_post, w_in, b_gate, ssm_a_re, ssm_a_im, ssm_log_dt, ssm_b_re, ssm_b_im, ssm_c_re, ssm_c_im, ssm_d, ssm_w_glu, ssm_b_glu, w_branch_attn, w_branch_ssm, w_out, norm_xa_pre, norm_xa_post, norm_mem, xa_wq, xa_wk, xa_wv, xa_wo, norm_ffn_pre, norm_ffn_post, ffn_w_up, ffn_conv_w, ffn_conv_b, ffn_w_down):
    p = dict(norm_mix_pre=norm_mix_pre, norm_mix_post=norm_mix_post, w_in=w_in, b_gate=b_gate,
             ssm_a_re=ssm_a_re, ssm_a_im=ssm_a_im, ssm_log_dt=ssm_log_dt, ssm_b_re=ssm_b_re,
             ssm_b_im=ssm_b_im, ssm_c_re=ssm_c_re, ssm_c_im=ssm_c_im, ssm_d=ssm_d,
             ssm_w_glu=ssm_w_glu, ssm_b_glu=ssm_b_glu, w_branch_attn=w_branch_attn,
             w_branch_ssm=w_branch_ssm, w_out=w_out, norm_xa_pre=norm_xa_pre,
             norm_xa_post=norm_xa_post, norm_mem=norm_mem, xa_wq=xa_wq, xa_wk=xa_wk, xa_wv=xa_wv,
             xa_wo=xa_wo, norm_ffn_pre=norm_ffn_pre, norm_ffn_post=norm_ffn_post,
             ffn_w_up=ffn_w_up, ffn_conv_w=ffn_conv_w, ffn_conv_b=ffn_conv_b, ffn_w_down=ffn_w_down)
    for l in range(w_in.shape[0]):
        x = _layer(x, mem, l, p)
    return x
```
